```python
import math
import jax, jax.numpy as jnp
from jax import lax
import numpy as np

D_MODEL = 2048
BATCH = 8
SEQ = 2048
DEPTH = 1
DEC_BATCH = 128
DEC_SEQ = 1
PAST_LEN = 2048
PAGE_SIZE = 128

N_HEADS = 16
HEAD_DIM = D_MODEL // N_HEADS
ATT_WIDTH = N_HEADS * HEAD_DIM
CONV_WIDTH = D_MODEL
CONV_K = 31
MOBA_BLOCK = 256
MOBA_TOPK = 3
Q_CHUNK = 128
N_BUCKETS = 32
MAX_DISTANCE = 128
D_FF = 5632
EPS = 1e-6
NEG = -1e30
SCALE = HEAD_DIM ** -0.5
IN_WIDTH = 3 * ATT_WIDTH + 2 * CONV_WIDTH + 2 * D_MODEL

kernel_name = 'moba_conformer_gated_hybrid_step'


def rmsnorm(x, g):
    xf = x.astype(jnp.float32)
    y = xf * lax.rsqrt(jnp.mean(xf * xf, axis=-1, keepdims=True) + EPS)
    return (y * g.astype(jnp.float32)).astype(x.dtype)


def layernorm(x, g, b):
    xf = x.astype(jnp.float32)
    mu = jnp.mean(xf, axis=-1, keepdims=True)
    var = jnp.mean(jnp.square(xf - mu), axis=-1, keepdims=True)
    y = (xf - mu) * lax.rsqrt(var + EPS)
    return (y * g.astype(jnp.float32) + b.astype(jnp.float32)).astype(x.dtype)


def swiglu_half(x, g, w_gate, w_up, w_down):
    h = rmsnorm(x, g)
    return 0.5 * ((jax.nn.silu(h @ w_gate) * (h @ w_up)) @ w_down)


def t5_bucket(dist):
    n = jnp.maximum(dist, 0)
    max_exact = N_BUCKETS // 2
    ratio = jnp.log(jnp.maximum(n, 1).astype(jnp.float32) / max_exact) / math.log(MAX_DISTANCE / max_exact)
    large = max_exact + (ratio * (N_BUCKETS - max_exact)).astype(jnp.int32)
    large = jnp.minimum(large, N_BUCKETS - 1)
    return jnp.where(n < max_exact, n, large)


def moba_one_seq(q, k, v, q_pos, rel_bias):
    L = k.shape[0]
    nb = -(-L // MOBA_BLOCK)
    pad = nb * MOBA_BLOCK - L
    kb = jnp.pad(k, ((0, pad), (0, 0), (0, 0))).reshape(nb, MOBA_BLOCK, N_HEADS, HEAD_DIM).transpose(2, 0, 1, 3)
    vb = jnp.pad(v, ((0, pad), (0, 0), (0, 0))).reshape(nb, MOBA_BLOCK, N_HEADS, HEAD_DIM).transpose(2, 0, 1, 3)
    kmean = jnp.mean(kb.astype(jnp.float32), axis=2)
    k_sel = min(MOBA_TOPK, nb)
    head_ix = jnp.arange(N_HEADS)[:, None, None, None]
    offs = jnp.arange(MOBA_BLOCK, dtype=jnp.int32)
    bias_tab = rel_bias.T

    def attend(qc, pc):
        tq = qc.shape[0]
        qh = qc.transpose(1, 0, 2)
        qblk = pc // MOBA_BLOCK
        gate = jnp.einsum('hqd,hnd->hqn', qh.astype(jnp.float32), kmean)
        past = jnp.arange(nb)[None, :] < qblk[:, None]
        gate = jnp.where(past[None], gate, NEG)
        _, sel = lax.top_k(gate, k_sel)
        sel_ok = jnp.arange(k_sel)[None, :] < qblk[:, None]
        own = jnp.broadcast_to(qblk[None, :, None], (N_HEADS, tq, 1))
        idx = jnp.concatenate([sel.astype(jnp.int32), own.astype(jnp.int32)], axis=-1)
        blk_ok = jnp.concatenate([sel_ok, jnp.ones((tq, 1), dtype=bool)], axis=-1)
        kg = jax.vmap(lambda kh, ih: kh[ih])(kb, idx)
        vg = jax.vmap(lambda vh, ih: vh[ih])(vb, idx)
        kpos = idx[..., None] * MOBA_BLOCK + offs
        dist = pc[None, :, None, None] - kpos
        ok = (dist >= 0) & blk_ok[None, :, :, None]
        bias = bias_tab[head_ix, t5_bucket(dist)].astype(jnp.float32)
        logits = jnp.einsum('hqd,hqskd->hqsk', qh, kg).astype(jnp.float32) * SCALE + bias
        logits = jnp.where(ok, logits, NEG).reshape(N_HEADS, tq, -1)
        p = jax.nn.softmax(logits, axis=-1).reshape(N_HEADS, tq, k_sel + 1, MOBA_BLOCK).astype(vg.dtype)
        o = jnp.einsum('hqsk,hqskd->hqd', p, vg)
        return o.transpose(1, 0, 2)

    tq = q.shape[0]
    if tq > Q_CHUNK and tq % Q_CHUNK == 0:
        nc = tq // Q_CHUNK
        out = lax.map(lambda a: attend(a[0], a[1]),
                      (q.reshape(nc, Q_CHUNK, N_HEADS, HEAD_DIM), q_pos.reshape(nc, Q_CHUNK)))
        return out.reshape(tq, N_HEADS, HEAD_DIM)
    return attend(q, q_pos)


def conv_module(u2, hist, dw_w, dw_b, ln_g, ln_b, w_o):
    a, g = jnp.split(u2, 2, axis=-1)
    u = a * jax.nn.sigmoid(g)
    full = jnp.concatenate([hist.astype(u.dtype), u], axis=1)
    y = lax.conv_general_dilated(full, dw_w[:, None, :].astype(u.dtype), window_strides=(1,), padding='VALID',
                                 dimension_numbers=('NWC', 'WIO', 'NWC'),
                                 feature_group_count=CONV_WIDTH) + dw_b
    y = jax.nn.silu(layernorm(y, ln_g, ln_b))
    return y @ w_o, full[:, -(CONV_K - 1):]


def layer(x, conv_hist, attend_fn, ffn1_norm, ffn1_w_gate, ffn1_w_up, ffn1_w_down, mix_norm, w_in, q_norm,
          k_norm, conv_dw_w, conv_dw_b, conv_ln_g, conv_ln_b, w_out_att, w_out_conv, w_out, ffn2_norm,
          ffn2_w_gate, ffn2_w_up, ffn2_w_down):
    x = x + swiglu_half(x, ffn1_norm, ffn1_w_gate, ffn1_w_up, ffn1_w_down)
    b, t, _ = x.shape
    h = rmsnorm(x, mix_norm)
    proj = h @ w_in
    cuts = [ATT_WIDTH, 2 * ATT_WIDTH, 3 * ATT_WIDTH, 3 * ATT_WIDTH + 2 * CONV_WIDTH,
            3 * ATT_WIDTH + 2 * CONV_WIDTH + D_MODEL]
    q, k, v, u2, ga, gb = jnp.split(proj, cuts, axis=-1)
    q = rmsnorm(q.reshape(b, t, N_HEADS, HEAD_DIM), q_norm)
    k = rmsnorm(k.reshape(b, t, N_HEADS, HEAD_DIM), k_norm)
    v = v.reshape(b, t, N_HEADS, HEAD_DIM)
    att = attend_fn(q, k, v)
    a_out = att.reshape(b, t, ATT_WIDTH) @ w_out_att
    c_out, new_conv = conv_module(u2, conv_hist, conv_dw_w, conv_dw_b, conv_ln_g, conv_ln_b, w_out_conv)
    merged = jax.nn.sigmoid(ga) * a_out + jax.nn.sigmoid(gb) * c_out
    x = x + merged @ w_out
    x = x + swiglu_half(x, ffn2_norm, ffn2_w_gate, ffn2_w_up, ffn2_w_down)
    return x, k, v, new_conv


def setup_inputs(seed: int = 0) -> dict:
    key = jax.random.key(seed)
    ks = iter(jax.random.split(key, 40))

    def nrm(shape, scale):
        return jax.random.normal(next(ks), shape, jnp.float32) * scale

    n_pages = PAST_LEN // PAGE_SIZE
    n_used = DEC_BATCH * n_pages
    n_phys = n_used + max(1, n_used // 4)
    x_prompt = nrm((BATCH, SEQ, D_MODEL), 1.0)
    x_sample = nrm((DEC_BATCH, DEC_SEQ, D_MODEL), 1.0)
    cache_k = nrm((DEPTH, n_phys, PAGE_SIZE, N_HEADS, HEAD_DIM), 1.0)
    cache_v = nrm((DEPTH, n_phys, PAGE_SIZE, N_HEADS, HEAD_DIM), 1.0)
    state_conv = nrm((DEPTH, DEC_BATCH, CONV_K - 1, CONV_WIDTH), 0.5)
    page_table = jax.random.permutation(next(ks), n_phys)[:n_used].astype(jnp.int32).reshape(DEC_BATCH, n_pages)
    return {
        'x_prompt': x_prompt,
        'x_sample': x_sample,
        'cache_k': cache_k,
        'cache_v': cache_v,
        'state_conv': state_conv,
        'page_table': page_table,
        'ffn1_norm': 1.0 + nrm((DEPTH, D_MODEL), 0.02),
        'ffn1_w_gate': nrm((DEPTH, D_MODEL, D_FF), D_MODEL ** -0.5),
        'ffn1_w_up': nrm((DEPTH, D_MODEL, D_FF), D_MODEL ** -0.5),
        'ffn1_w_down': nrm((DEPTH, D_FF, D_MODEL), D_FF ** -0.5),
        'mix_norm': 1.0 + nrm((DEPTH, D_MODEL), 0.02),
        'w_in': nrm((DEPTH, D_MODEL, IN_WIDTH), D_MODEL ** -0.5),
        'q_norm': 1.0 + nrm((DEPTH, HEAD_DIM), 0.02),
        'k_norm': 1.0 + nrm((DEPTH, HEAD_DIM), 0.02),
        'rel_bias': nrm((N_BUCKETS, N_HEADS), 0.5),
        'conv_dw_w': nrm((DEPTH, CONV_K, CONV_WIDTH), CONV_K ** -0.5),
        'conv_dw_b': nrm((DEPTH, CONV_WIDTH), 0.02),
        'conv_ln_g': 1.0 + nrm((DEPTH, CONV_WIDTH), 0.02),
        'conv_ln_b': nrm((DEPTH, CONV_WIDTH), 0.02),
        'w_out_att': nrm((DEPTH, ATT_WIDTH, D_MODEL), ATT_WIDTH ** -0.5),
        'w_out_conv': nrm((DEPTH, CONV_WIDTH, D_MODEL), CONV_WIDTH ** -0.5),
        'w_out': nrm((DEPTH, D_MODEL, D_MODEL), D_MODEL ** -0.5),
        'ffn2_norm': 1.0 + nrm((DEPTH, D_MODEL), 0.02),
        'ffn2_w_gate': nrm((DEPTH, D_MODEL, D_FF), D_MODEL ** -0.5),
        'ffn2_w_up': nrm((DEPTH, D_MODEL, D_FF), D_MODEL ** -0.5),
        'ffn2_w_down': nrm((DEPTH, D_FF, D_MODEL), D_FF ** -0.5),
    }


def reference(x_prompt, x_sample, cache_k, cache_v, state_conv, page_table, ffn1_norm, ffn1_w_gate, ffn1_w_up,
              ffn1_w_down, mix_norm, w_in, q_norm, k_norm, rel_bias, conv_dw_w, conv_dw_b, conv_ln_g, conv_ln_b,
              w_out_att, w_out_conv, w_out, ffn2_norm, ffn2_w_gate, ffn2_w_up, ffn2_w_down):
    seq = x_prompt.shape[1]
    past_len = page_table.shape[1] * PAGE_SIZE
    pos_p = jnp.arange(seq, dtype=jnp.int32)
    pos_s = past_len + jnp.arange(x_sample.shape[1], dtype=jnp.int32)
    yp, ys = x_prompt, x_sample
    kp, vp, cp, kss, vss, css = [], [], [], [], [], []
    for l in range(DEPTH):
        lw = (ffn1_norm[l], ffn1_w_gate[l], ffn1_w_up[l], ffn1_w_down[l], mix_norm[l], w_in[l], q_norm[l],
              k_norm[l], conv_dw_w[l], conv_dw_b[l], conv_ln_g[l], conv_ln_b[l], w_out_att[l], w_out_conv[l],
              w_out[l], ffn2_norm[l], ffn2_w_gate[l], ffn2_w_up[l], ffn2_w_down[l])

        def attend_prompt(q, k, v):
            return lax.map(lambda a: moba_one_seq(a[0], a[1], a[2], pos_p, rel_bias), (q, k, v))

        def attend_sample(q, k, v, ck=cache_k[l], cv=cache_v[l]):
            def one(a):
                qq, kn, vn, pt = a
                kk = jnp.concatenate([ck[pt].reshape(-1, N_HEADS, HEAD_DIM), kn], axis=0)
                vv = jnp.concatenate([cv[pt].reshape(-1, N_HEADS, HEAD_DIM), vn], axis=0)
                return moba_one_seq(qq, kk, vv, pos_s, rel_bias)
            return lax.map(one, (q, k, v, page_table))

        zero_hist = jnp.zeros((yp.shape[0], CONV_K - 1, CONV_WIDTH), yp.dtype)
        yp, k1, v1, c1 = layer(yp, zero_hist, attend_prompt, *lw)
        ys, k2, v2, c2 = layer(ys, state_conv[l], attend_sample, *lw)
        kp.append(k1)
        vp.append(v1)
        cp.append(c1)
        kss.append(k2)
        vss.append(v2)
        css.append(c2)
    return (yp, ys, jnp.stack(kp), jnp.stack(vp), jnp.stack(cp), jnp.stack(kss), jnp.stack(vss), jnp.stack(css))
```

```python
import functools
import math

import numpy as np
import jax
import jax.numpy as jnp
from jax import lax
from jax.experimental import pallas as pl
from jax.experimental.pallas import tpu as pltpu

D_MODEL = 2048
N_HEADS = 16
HEAD_DIM = 128
D_FF = 5632
CONV_K = 31
MOBA_BLOCK = 256
MOBA_TOPK = 3
N_BUCKETS = 32
MAX_DISTANCE = 128
PAGE_SIZE = 128
EPS = 1e-6
NEG = -1e30
SCALE = HEAD_DIM ** -0.5

F32 = jnp.float32
BF16 = jnp.bfloat16

VMEM_LIMIT_BYTES = 56 * 1024 * 1024
LANES = 128
CONV_HALO = 32


def _params(*sem):
    return pltpu.CompilerParams(dimension_semantics=sem, vmem_limit_bytes=VMEM_LIMIT_BYTES)


def _t5_bucket_np(n, dtype):
    n = np.maximum(n, 0)
    max_exact = N_BUCKETS // 2
    ratio = np.log(np.maximum(n, 1).astype(dtype) / dtype(max_exact)) / dtype(math.log(MAX_DISTANCE / max_exact))
    large = max_exact + (ratio * dtype(N_BUCKETS - max_exact)).astype(np.int32)
    large = np.minimum(large, N_BUCKETS - 1)
    return np.where(n < max_exact, n, large)


def _bucket_thresholds():
    d = np.arange(0, 8192)
    b32 = _t5_bucket_np(d, np.float32)
    b64 = _t5_bucket_np(d, np.float64)
    assert np.array_equal(b32, b64) and np.all(np.diff(b32) >= 0)
    assert np.all(b32[MAX_DISTANCE:] == N_BUCKETS - 1)
    return [int(np.argmax(b32 >= t)) for t in range(N_BUCKETS)]


BUCKET_THR = _bucket_thresholds()


def _rms(x, g):
    ms = jnp.mean(x * x, axis=-1, keepdims=True)
    return x * lax.rsqrt(ms + EPS) * g


def _ffn_kernel(*refs, emit_norm):
    if emit_norm:
        x_ref, g_ref, wg_ref, wu_ref, wd_ref, gn_ref, o_ref, hn_ref, h_scr = refs
    else:
        x_ref, g_ref, wg_ref, wu_ref, wd_ref, o_ref, h_scr = refs
    f = pl.program_id(1)

    @pl.when(f == 0)
    def _():
        h_scr[...] = _rms(x_ref[...], g_ref[...]).astype(BF16)
        o_ref[...] = jnp.zeros_like(o_ref)

    h = h_scr[...]
    gate = jnp.dot(h, wg_ref[...], preferred_element_type=F32)
    up = jnp.dot(h, wu_ref[...], preferred_element_type=F32)
    hid = (gate * jax.nn.sigmoid(gate) * up).astype(BF16)
    o_ref[...] += jnp.dot(hid, wd_ref[...], preferred_element_type=F32)

    @pl.when(f == pl.num_programs(1) - 1)
    def _():
        y = x_ref[...] + 0.5 * o_ref[...]
        o_ref[...] = y
        if emit_norm:
            hn_ref[...] = _rms(y, gn_ref[...]).astype(BF16)


def _ffn(x, g, wg, wu, wd, gn, tm, tf=512):
    m = x.shape[0]
    emit_norm = gn is not None
    row = pl.BlockSpec((tm, D_MODEL), lambda i, f: (i, 0))
    vec = pl.BlockSpec((1, D_MODEL), lambda i, f: (0, 0))
    in_specs = [row, vec,
                pl.BlockSpec((D_MODEL, tf), lambda i, f: (0, f)),
                pl.BlockSpec((D_MODEL, tf), lambda i, f: (0, f)),
                pl.BlockSpec((tf, D_MODEL), lambda i, f: (f, 0))]
    args = [x, g, wg, wu, wd]
    out_shape = [jax.ShapeDtypeStruct((m, D_MODEL), F32)]
    out_specs = [row]
    if emit_norm:
        in_specs.append(vec)
        args.append(gn)
        out_shape.append(jax.ShapeDtypeStruct((m, D_MODEL), BF16))
        out_specs.append(row)
    out = pl.pallas_call(
        functools.partial(_ffn_kernel, emit_norm=emit_norm),
        grid=(m // tm, D_FF // tf),
        in_specs=in_specs, out_specs=out_specs, out_shape=out_shape,
        scratch_shapes=[pltpu.VMEM((tm, D_MODEL), BF16)],
        compiler_params=_params("parallel", "arbitrary"),
        name="ffn",
    )(*args)
    return out if emit_norm else out[0]


def _proj_headnorm_kernel(h_ref, w_ref, g_ref, o_ref, *, scale):
    acc = jnp.dot(h_ref[...], w_ref[...], preferred_element_type=F32)
    g = g_ref[...] * scale
    for s in range(acc.shape[1] // HEAD_DIM):
        sl = slice(s * HEAD_DIM, (s + 1) * HEAD_DIM)
        o_ref[:, sl] = _rms(acc[:, sl], g).astype(o_ref.dtype)


def _proj_plain_kernel(h_ref, w_ref, o_ref):
    o_ref[...] = jnp.dot(h_ref[...], w_ref[...], preferred_element_type=F32).astype(o_ref.dtype)


def _proj_glu_kernel(h_ref, wa_ref, wg_ref, o_ref):
    h = h_ref[...]
    a = jnp.dot(h, wa_ref[...], preferred_element_type=F32)
    g = jnp.dot(h, wg_ref[...], preferred_element_type=F32)
    o_ref[...] = a * jax.nn.sigmoid(g)


def _proj(kind, h, w_in, col0, tm, out_dtype, gvec=None, scale=1.0, col1=None, tn=512):
    m = h.shape[0]
    hspec = pl.BlockSpec((tm, D_MODEL), lambda i, j: (i, 0))
    ospec = pl.BlockSpec((tm, tn), lambda i, j: (i, j))

    def wspec(c0):
        return pl.BlockSpec((D_MODEL, tn), lambda i, j: (0, c0 // tn + j))

    if kind == "headnorm":
        body = functools.partial(_proj_headnorm_kernel, scale=scale)
        in_specs = [hspec, wspec(col0), pl.BlockSpec((1, HEAD_DIM), lambda i, j: (0, 0))]
        args = (h, w_in, gvec)
    elif kind == "glu":
        body = _proj_glu_kernel
        in_specs = [hspec, wspec(col0), wspec(col1)]
        args = (h, w_in, w_in)
    else:
        body = _proj_plain_kernel
        in_specs = [hspec, wspec(col0)]
        args = (h, w_in)
    return pl.pallas_call(
        body, grid=(m // tm, D_MODEL // tn), in_specs=in_specs, out_specs=ospec,
        out_shape=jax.ShapeDtypeStruct((m, D_MODEL), out_dtype),
        compiler_params=_params("parallel", "arbitrary"),
        name="proj_" + kind,
    )(*args)


def _bias_chain(d, tab):
    val = tab(0)
    for t in range(1, N_BUCKETS):
        val = jnp.where(d >= BUCKET_THR[t], tab(t), val)
    return val


def _prompt_bias_kernel(rb_ref, o_ref):
    h = pl.program_id(0)
    shape = (MOBA_BLOCK, MOBA_BLOCK)
    d = lax.broadcasted_iota(jnp.int32, shape, 1) - lax.broadcasted_iota(jnp.int32, shape, 0)
    tab = lambda t: rb_ref[t, h]
    own = _bias_chain(jnp.maximum(d, 0), tab)
    o_ref[0, 0] = jnp.where(d >= 0, own, NEG)
    o_ref[0, 1] = _bias_chain(d + MOBA_BLOCK, tab)
    o_ref[0, 2] = jnp.full(shape, rb_ref[N_BUCKETS - 1, h], F32)


def _prompt_bias(rel_bias):
    return pl.pallas_call(
        _prompt_bias_kernel, grid=(N_HEADS,),
        in_specs=[pl.BlockSpec(memory_space=pltpu.SMEM)],
        out_specs=pl.BlockSpec((1, 3, MOBA_BLOCK, MOBA_BLOCK), lambda h: (h, 0, 0, 0)),
        out_shape=jax.ShapeDtypeStruct((N_HEADS, 3, MOBA_BLOCK, MOBA_BLOCK), F32),
        compiler_params=_params("arbitrary"),
        name="prompt_bias",
    )(rel_bias)


def _sample_bias_kernel(rbt_ref, o_ref, *, q_pos):
    n_pages = o_ref.shape[0]
    tab = lambda t: rbt_ref[:, t:t + 1]
    for p in range(n_pages):
        d = q_pos - (p * PAGE_SIZE + lax.broadcasted_iota(jnp.int32, (N_HEADS, PAGE_SIZE), 1))
        o_ref[p] = jnp.broadcast_to(_bias_chain(d, tab), (N_HEADS, PAGE_SIZE))


def _sample_bias(rel_bias_t, n_pages):
    return pl.pallas_call(
        functools.partial(_sample_bias_kernel, q_pos=n_pages * PAGE_SIZE),
        out_shape=jax.ShapeDtypeStruct((n_pages, N_HEADS, PAGE_SIZE), F32),
        name="sample_bias",
    )(rel_bias_t)


def _topk_select(gates, k_sel):
    n = len(gates)
    if n <= k_sel:
        return [None] * n
    sel = []
    for j in range(n):
        cnt = jnp.zeros(gates[j].shape, F32)
        for j2 in range(n):
            if j2 == j:
                continue
            ahead = (gates[j2] >= gates[j]) if j2 < j else (gates[j2] > gates[j])
            cnt = cnt + jnp.where(ahead, 1.0, 0.0)
        sel.append(cnt < k_sel)
    return sel


def _prompt_attn_kernel(q_ref, k_ref, v_ref, bias_ref, o_ref, kb_scr, vt_scr, s_scr, p_scr):
    t = q_ref.shape[1]
    nb = t // MOBA_BLOCK
    kb_scr[...] = k_ref[0].astype(BF16)
    vt_scr[...] = v_ref[0].T.astype(BF16)
    for i in range(nb):
        nk = (i + 1) * MOBA_BLOCK
        qi = q_ref[0, i * MOBA_BLOCK:(i + 1) * MOBA_BLOCK, :]
        s_scr[0:nk, :] = lax.dot_general(kb_scr[0:nk, :], qi, (((1,), (1,)), ((), ())),
                                         preferred_element_type=F32)
        blk = lambda j: slice(j * MOBA_BLOCK, (j + 1) * MOBA_BLOCK)
        gates = [jnp.sum(s_scr[blk(j), :], axis=0, keepdims=True) for j in range(i)]
        sel = _topk_select(gates, MOBA_TOPK)
        m = jnp.full((1, MOBA_BLOCK), NEG, F32)
        for j in range(i + 1):
            kind = 0 if j == i else (1 if j == i - 1 else 2)
            tile = s_scr[blk(j), :] + bias_ref[0, kind]
            if j < i and sel[j] is not None:
                tile = tile + jnp.where(sel[j], 0.0, NEG)
            s_scr[blk(j), :] = tile
            m = jnp.maximum(m, jnp.max(tile, axis=0, keepdims=True))
        l = jnp.zeros((1, MOBA_BLOCK), F32)
        for j in range(i + 1):
            p = jnp.exp(s_scr[blk(j), :] - m)
            l = l + jnp.sum(p, axis=0, keepdims=True)
            p_scr[blk(j), :] = p.astype(BF16)
        ot = jnp.dot(vt_scr[:, 0:nk], p_scr[0:nk, :], preferred_element_type=F32)
        o_ref[0, i * MOBA_BLOCK:(i + 1) * MOBA_BLOCK, :] = (ot / l).T.astype(o_ref.dtype)


def _prompt_attn(q, k, v, bias):
    b, t, _ = q.shape
    spec = pl.BlockSpec((1, t, HEAD_DIM), lambda bi, h: (bi, 0, h))
    return pl.pallas_call(
        _prompt_attn_kernel, grid=(b, N_HEADS),
        in_specs=[spec, spec, spec,
                  pl.BlockSpec((1, 3, MOBA_BLOCK, MOBA_BLOCK), lambda bi, h: (h, 0, 0, 0))],
        out_specs=spec,
        out_shape=jax.ShapeDtypeStruct((b, t, N_HEADS * HEAD_DIM), BF16),
        scratch_shapes=[pltpu.VMEM((t, HEAD_DIM), BF16), pltpu.VMEM((HEAD_DIM, t), BF16),
                        pltpu.VMEM((t, MOBA_BLOCK), F32), pltpu.VMEM((t, MOBA_BLOCK), BF16)],
        compiler_params=_params("parallel", "arbitrary"),
        name="prompt_attn",
    )(q, k, v, bias)


def _head_diag_mask(width):
    shape = (N_HEADS, width)
    return lax.broadcasted_iota(jnp.int32, shape, 1) // HEAD_DIM == lax.broadcasted_iota(jnp.int32, shape, 0)


def _sample_attn_kernel(pt_ref, q_ref, kn_ref, vn_ref, k_ref, v_ref, bias_ref, rbt_ref, o_ref,
                        m_scr, l_scr, g_scr, o_scr):
    del pt_ref
    p = pl.program_id(1)
    n_pages = pl.num_programs(1)
    width = N_HEADS * HEAD_DIM
    diag = _head_diag_mask(width)
    qbd = jnp.where(diag, jnp.broadcast_to(q_ref[0], (N_HEADS, width)), 0.0)

    s = lax.dot_general(qbd.astype(BF16), k_ref[0].astype(BF16), (((1,), (1,)), ((), ())),
                        preferred_element_type=F32)
    bshape = (N_HEADS, PAGE_SIZE)
    g_scr[p] = jnp.broadcast_to(jnp.sum(s, axis=-1, keepdims=True), bshape)
    logits = s + bias_ref[0]
    m = jnp.max(logits, axis=-1, keepdims=True)
    e = jnp.exp(logits - m)
    m_scr[p] = jnp.broadcast_to(m, bshape)
    l_scr[p] = jnp.broadcast_to(jnp.sum(e, axis=-1, keepdims=True), bshape)
    pv = jnp.dot(e.astype(BF16), v_ref[0].astype(BF16), preferred_element_type=F32)
    o_scr[p] = jnp.where(diag, pv, 0.0)

    @pl.when(p == n_pages - 1)
    def _():
        pages_per_block = MOBA_BLOCK // PAGE_SIZE
        n_blocks = o_scr.shape[0] // pages_per_block
        gates = [sum(g_scr[j * pages_per_block + r] for r in range(pages_per_block)) for j in range(n_blocks)]
        sel = _topk_select(gates, MOBA_TOPK)
        s_own = jnp.sum(qbd * kn_ref[0], axis=-1, keepdims=True) + rbt_ref[:, 0:1]
        m_tot = jnp.broadcast_to(s_own, bshape)
        for j in range(n_blocks):
            for r in range(pages_per_block):
                mp = m_scr[j * pages_per_block + r]
                if sel[j] is not None:
                    mp = jnp.where(sel[j], mp, NEG)
                m_tot = jnp.maximum(m_tot, mp)
        w_own = jnp.exp(s_own - m_tot)
        l_tot = w_own
        tile = lambda w: jnp.concatenate([w] * (width // PAGE_SIZE), axis=-1)
        acc = jnp.where(diag, tile(w_own) * vn_ref[0], 0.0)
        for j in range(n_blocks):
            for r in range(pages_per_block):
                pg = j * pages_per_block + r
                w = jnp.exp(m_scr[pg] - m_tot)
                if sel[j] is not None:
                    w = jnp.where(sel[j], w, 0.0)
                l_tot = l_tot + w * l_scr[pg]
                acc = acc + tile(w) * o_scr[pg]
        out = acc / tile(l_tot)
        o_ref[0] = jnp.sum(out, axis=0, keepdims=True)


def _sample_attn(page_table, q, k_new, v_new, cache_k, cache_v, bias, rel_bias_t):
    s, n_pages = page_table.shape
    width = N_HEADS * HEAD_DIM
    row = pl.BlockSpec((1, 1, width), lambda si, p, pt: (si, 0, 0))
    page = pl.BlockSpec((1, PAGE_SIZE, width), lambda si, p, pt: (pt[si, p], 0, 0))
    grid_spec = pltpu.PrefetchScalarGridSpec(
        num_scalar_prefetch=1, grid=(s, n_pages),
        in_specs=[row, row, row, page, page,
                  pl.BlockSpec((1, N_HEADS, PAGE_SIZE), lambda si, p, pt: (p, 0, 0)),
                  pl.BlockSpec((N_HEADS, N_BUCKETS), lambda si, p, pt: (0, 0))],
        out_specs=row,
        scratch_shapes=[pltpu.VMEM((n_pages, N_HEADS, PAGE_SIZE), F32)] * 3
        + [pltpu.VMEM((n_pages, N_HEADS, width), F32)],
    )
    return pl.pallas_call(
        _sample_attn_kernel, grid_spec=grid_spec,
        out_shape=jax.ShapeDtypeStruct((s, 1, width), F32),
        compiler_params=_params("parallel", "arbitrary"),
        name="sample_attn",
    )(page_table, q, k_new, v_new, cache_k, cache_v, bias, rel_bias_t)


def _ln_swish(y, g, b):
    mu = jnp.mean(y, axis=-1, keepdims=True)
    yc = y - mu
    var = jnp.mean(yc * yc, axis=-1, keepdims=True)
    z = yc * lax.rsqrt(var + EPS) * g + b
    return z * jax.nn.sigmoid(z)


def _conv_prompt_kernel(cur_ref, halo_ref, w_ref, b_ref, g_ref, beta_ref, o_ref, xx_scr, y_scr):
    tt = cur_ref.shape[1]
    first = pl.program_id(1) == 0
    xx_scr[0:CONV_HALO, :] = jnp.where(first, 0.0, halo_ref[0])
    xx_scr[CONV_HALO:, :] = cur_ref[0]
    lead = CONV_HALO - (CONV_K - 1)

    def body(cc, carry):
        cols = pl.ds(pl.multiple_of(cc * LANES, LANES), LANES)
        acc = jnp.zeros((tt, LANES), F32)
        for j in range(CONV_K):
            acc = acc + w_ref[j:j + 1, cols] * xx_scr[lead + j:lead + j + tt, cols]
        y_scr[:, cols] = acc + b_ref[:, cols]
        return carry

    lax.fori_loop(0, cur_ref.shape[2] // LANES, body, 0)
    o_ref[0] = _ln_swish(y_scr[...], g_ref[...], beta_ref[...]).astype(o_ref.dtype)


def _conv_prompt(u, w, b, g, beta, tt=256):
    bsz, t, c = u.shape
    vec = pl.BlockSpec((1, c), lambda bi, i: (0, 0))
    halo_blocks = tt // CONV_HALO
    return pl.pallas_call(
        _conv_prompt_kernel, grid=(bsz, t // tt),
        in_specs=[pl.BlockSpec((1, tt, c), lambda bi, i: (bi, i, 0)),
                  pl.BlockSpec((1, CONV_HALO, c), lambda bi, i: (bi, jnp.maximum(i * halo_blocks - 1, 0), 0)),
                  pl.BlockSpec((CONV_K, c), lambda bi, i: (0, 0)), vec, vec, vec],
        out_specs=pl.BlockSpec((1, tt, c), lambda bi, i: (bi, i, 0)),
        out_shape=jax.ShapeDtypeStruct((bsz, t, c), BF16),
        scratch_shapes=[pltpu.VMEM((CONV_HALO + tt, c), F32), pltpu.VMEM((tt, c), F32)],
        compiler_params=_params("parallel", "arbitrary"),
        name="conv_prompt",
    )(u, u, w, b, g, beta)


def _conv_sample_kernel(st_ref, u_ref, w_ref, b_ref, g_ref, beta_ref, y_ref, ns_ref):
    st = st_ref[...]
    u = u_ref[...]
    hist = CONV_K - 1
    y = jnp.sum(st * w_ref[0:hist, :][None], axis=1, keepdims=True) + u * w_ref[hist:CONV_K, :][None]
    y = y + b_ref[...][None]
    y_ref[...] = _ln_swish(y, g_ref[...][None], beta_ref[...][None])
    ns_ref[:, 0:hist - 1, :] = st_ref[:, 1:hist, :]
    ns_ref[:, hist - 1:hist, :] = u


def _conv_sample(state, u, w, b, g, beta, sb=16):
    s, hist, c = state.shape
    vec = pl.BlockSpec((1, c), lambda i: (0, 0))
    return pl.pallas_call(
        _conv_sample_kernel, grid=(s // sb,),
        in_specs=[pl.BlockSpec((sb, hist, c), lambda i: (i, 0, 0)),
                  pl.BlockSpec((sb, 1, c), lambda i: (i, 0, 0)),
                  pl.BlockSpec((CONV_K, c), lambda i: (0, 0)), vec, vec, vec],
        out_specs=[pl.BlockSpec((sb, 1, c), lambda i: (i, 0, 0)),
                   pl.BlockSpec((sb, hist, c), lambda i: (i, 0, 0))],
        out_shape=[jax.ShapeDtypeStruct((s, 1, c), F32), jax.ShapeDtypeStruct((s, hist, c), F32)],
        compiler_params=_params("parallel"),
        name="conv_sample",
    )(state, u, w, b, g, beta)


def _mix_kernel(x_ref, att_ref, yc_ref, h_ref, woa_ref, woc_ref, wga_ref, wgb_ref, wo_ref, o_ref):
    c = pl.program_id(1)

    @pl.when(c == 0)
    def _():
        o_ref[...] = x_ref[...]

    h = h_ref[...]
    a_out = jnp.dot(att_ref[...], woa_ref[...], preferred_element_type=F32)
    c_out = jnp.dot(yc_ref[...], woc_ref[...], preferred_element_type=F32)
    ga = jnp.dot(h, wga_ref[...], preferred_element_type=F32)
    gb = jnp.dot(h, wgb_ref[...], preferred_element_type=F32)
    merged = (jax.nn.sigmoid(ga) * a_out + jax.nn.sigmoid(gb) * c_out).astype(BF16)
    o_ref[...] += jnp.dot(merged, wo_ref[...], preferred_element_type=F32)


def _mix(x, att, yc, h, w_out_att, w_out_conv, w_in, ga_col0, gb_col0, w_out, tm, tn=256):
    m = x.shape[0]
    row = pl.BlockSpec((tm, D_MODEL), lambda i, c: (i, 0))

    def wcol(c0):
        return pl.BlockSpec((D_MODEL, tn), lambda i, c: (0, c0 // tn + c))

    return pl.pallas_call(
        _mix_kernel, grid=(m // tm, D_MODEL // tn),
        in_specs=[row, row, row, row, wcol(0), wcol(0), wcol(ga_col0), wcol(gb_col0),
                  pl.BlockSpec((tn, D_MODEL), lambda i, c: (c, 0))],
        out_specs=row,
        out_shape=jax.ShapeDtypeStruct((m, D_MODEL), F32),
        compiler_params=_params("parallel", "arbitrary"),
        name="mix",
    )(x, att, yc, h, w_out_att, w_out_conv, w_in, w_in, w_out)


def _layer(x, tm, w, attend_fn, conv_fn):
    x1, h = _ffn(x, w["ffn1_norm"], w["ffn1_w_gate"], w["ffn1_w_up"], w["ffn1_w_down"], w["mix_norm"], tm)
    q = _proj("headnorm", h, w["w_in"], 0, tm, BF16, gvec=w["q_norm"], scale=SCALE)
    k = _proj("headnorm", h, w["w_in"], D_MODEL, tm, F32, gvec=w["k_norm"])
    v = _proj("plain", h, w["w_in"], 2 * D_MODEL, tm, F32)
    u = _proj("glu", h, w["w_in"], 3 * D_MODEL, tm, F32, col1=4 * D_MODEL)
    att = attend_fn(q, k, v)
    yc, conv_state = conv_fn(u)
    x2 = _mix(x1, att, yc, h, w["w_out_att"], w["w_out_conv"], w["w_in"], 5 * D_MODEL, 6 * D_MODEL,
              w["w_out"], tm)
    y = _ffn(x2, w["ffn2_norm"], w["ffn2_w_gate"], w["ffn2_w_up"], w["ffn2_w_down"], None, tm)
    return y, k, v, conv_state


def kernel(x_prompt, x_sample, cache_k, cache_v, state_conv, page_table, ffn1_norm, ffn1_w_gate, ffn1_w_up,
           ffn1_w_down, mix_norm, w_in, q_norm, k_norm, rel_bias, conv_dw_w, conv_dw_b, conv_ln_g, conv_ln_b,
           w_out_att, w_out_conv, w_out, ffn2_norm, ffn2_w_gate, ffn2_w_up, ffn2_w_down):
    depth = ffn1_norm.shape[0]
    assert depth == 1 and x_sample.shape[1] == 1
    bsz, seq, d = x_prompt.shape
    n_s = x_sample.shape[0]
    n_pages = page_table.shape[1]
    hist = CONV_K - 1
    l = 0
    w = {
        "ffn1_norm": ffn1_norm[l][None], "mix_norm": mix_norm[l][None], "ffn2_norm": ffn2_norm[l][None],
        "q_norm": q_norm[l][None], "k_norm": k_norm[l][None],
        "ffn1_w_gate": ffn1_w_gate[l].astype(BF16), "ffn1_w_up": ffn1_w_up[l].astype(BF16),
        "ffn1_w_down": ffn1_w_down[l].astype(BF16), "w_in": w_in[l].astype(BF16),
        "w_out_att": w_out_att[l].astype(BF16), "w_out_conv": w_out_conv[l].astype(BF16),
        "w_out": w_out[l].astype(BF16),
        "ffn2_w_gate": ffn2_w_gate[l].astype(BF16), "ffn2_w_up": ffn2_w_up[l].astype(BF16),
        "ffn2_w_down": ffn2_w_down[l].astype(BF16),
    }
    dw_w, dw_b = conv_dw_w[l], conv_dw_b[l][None]
    ln_g, ln_b = conv_ln_g[l][None], conv_ln_b[l][None]

    bias_p = _prompt_bias(rel_bias)

    def attend_prompt(q, k, v):
        shp = (bsz, seq, d)
        return _prompt_attn(q.reshape(shp), k.reshape(shp), v.reshape(shp), bias_p).reshape(bsz * seq, d)

    def conv_prompt(u):
        u3 = u.reshape(bsz, seq, d)
        yc = _conv_prompt(u3, dw_w, dw_b, ln_g, ln_b).reshape(bsz * seq, d)
        return yc, u3[:, seq - hist:, :]

    yp, kp, vp, cp = _layer(x_prompt.reshape(bsz * seq, d), 512, w, attend_prompt, conv_prompt)

    rel_bias_t = rel_bias.T
    bias_s = _sample_bias(rel_bias_t, n_pages)
    ck = cache_k[l].reshape(cache_k.shape[1], PAGE_SIZE, d)
    cv = cache_v[l].reshape(cache_v.shape[1], PAGE_SIZE, d)

    def attend_sample(q, k, v):
        shp = (n_s, 1, d)
        att = _sample_attn(page_table, q.astype(F32).reshape(shp), k.reshape(shp), v.reshape(shp), ck, cv,
                           bias_s, rel_bias_t)
        return att.reshape(n_s, d).astype(BF16)

    def conv_sample(u):
        yc, ns = _conv_sample(state_conv[l], u.reshape(n_s, 1, d), dw_w, dw_b, ln_g, ln_b)
        return yc.reshape(n_s, d).astype(BF16), ns

    ys, ks, vs, cs = _layer(x_sample.reshape(n_s, d), n_s, w, attend_sample, conv_sample)

    heads = (N_HEADS, HEAD_DIM)
    return (yp.reshape(bsz, seq, d), ys.reshape(n_s, 1, d),
            kp.reshape(1, bsz, seq, *heads), vp.reshape(1, bsz, seq, *heads), cp[None],
            ks.reshape(1, n_s, 1, *heads), vs.reshape(1, n_s, 1, *heads), cs[None])
```

```python
import functools
import math

import numpy as np
import jax
import jax.numpy as jnp
from jax import lax
from jax.experimental import pallas as pl
from jax.experimental.pallas import tpu as pltpu

D_MODEL = 2048
N_HEADS = 16
HEAD_DIM = 128
D_FF = 5632
CONV_K = 31
MOBA_BLOCK = 256
MOBA_TOPK = 3
N_BUCKETS = 32
MAX_DISTANCE = 128
PAGE_SIZE = 128
EPS = 1e-6
NEG = -1e30
SCALE = HEAD_DIM ** -0.5

F32 = jnp.float32
BF16 = jnp.bfloat16

VMEM_LIMIT_BYTES = 56 * 1024 * 1024
LANES = 128
SUBLANES = 8
CONV_HALO = 32


def _params(*sem):
    return pltpu.CompilerParams(dimension_semantics=sem, vmem_limit_bytes=VMEM_LIMIT_BYTES)


def _t5_bucket_np(n, dtype):
    n = np.maximum(n, 0)
    max_exact = N_BUCKETS // 2
    ratio = np.log(np.maximum(n, 1).astype(dtype) / dtype(max_exact)) / dtype(math.log(MAX_DISTANCE / max_exact))
    large = max_exact + (ratio * dtype(N_BUCKETS - max_exact)).astype(np.int32)
    large = np.minimum(large, N_BUCKETS - 1)
    return np.where(n < max_exact, n, large)


def _bucket_thresholds():
    d = np.arange(0, 8192)
    b32 = _t5_bucket_np(d, np.float32)
    b64 = _t5_bucket_np(d, np.float64)
    assert np.array_equal(b32, b64) and np.all(np.diff(b32) >= 0)
    assert np.all(b32[MAX_DISTANCE:] == N_BUCKETS - 1)
    return [int(np.argmax(b32 >= t)) for t in range(N_BUCKETS)]


BUCKET_THR = _bucket_thresholds()


def _rms(x, g):
    ms = jnp.mean(x * x, axis=-1, keepdims=True)
    return x * lax.rsqrt(ms + EPS) * g


def _ffn_kernel(*refs, emit_norm):
    if emit_norm:
        x_ref, g_ref, wg_ref, wu_ref, wd_ref, gn_ref, o_ref, hn_ref, h_scr = refs
    else:
        x_ref, g_ref, wg_ref, wu_ref, wd_ref, o_ref, h_scr = refs
    f = pl.program_id(1)

    @pl.when(f == 0)
    def _():
        h_scr[...] = _rms(x_ref[...], g_ref[...]).astype(BF16)
        o_ref[...] = jnp.zeros_like(o_ref)

    h = h_scr[...]
    gate = jnp.dot(h, wg_ref[...], preferred_element_type=F32)
    up = jnp.dot(h, wu_ref[...], preferred_element_type=F32)
    hid = (gate * jax.nn.sigmoid(gate) * up).astype(BF16)
    o_ref[...] += jnp.dot(hid, wd_ref[...], preferred_element_type=F32)

    @pl.when(f == pl.num_programs(1) - 1)
    def _():
        y = x_ref[...] + 0.5 * o_ref[...]
        o_ref[...] = y
        if emit_norm:
            hn_ref[...] = _rms(y, gn_ref[...]).astype(BF16)


def _ffn(x, g, wg, wu, wd, gn, tm, tf=512):
    m = x.shape[0]
    emit_norm = gn is not None
    row = pl.BlockSpec((tm, D_MODEL), lambda i, f: (i, 0))
    vec = pl.BlockSpec((1, D_MODEL), lambda i, f: (0, 0))
    in_specs = [row, vec,
                pl.BlockSpec((D_MODEL, tf), lambda i, f: (0, f)),
                pl.BlockSpec((D_MODEL, tf), lambda i, f: (0, f)),
                pl.BlockSpec((tf, D_MODEL), lambda i, f: (f, 0))]
    args = [x, g, wg, wu, wd]
    out_shape = [jax.ShapeDtypeStruct((m, D_MODEL), F32)]
    out_specs = [row]
    if emit_norm:
        in_specs.append(vec)
        args.append(gn)
        out_shape.append(jax.ShapeDtypeStruct((m, D_MODEL), BF16))
        out_specs.append(row)
    out = pl.pallas_call(
        functools.partial(_ffn_kernel, emit_norm=emit_norm),
        grid=(m // tm, D_FF // tf),
        in_specs=in_specs, out_specs=out_specs, out_shape=out_shape,
        scratch_shapes=[pltpu.VMEM((tm, D_MODEL), BF16)],
        compiler_params=_params("parallel", "arbitrary"),
        name="ffn",
    )(*args)
    return out if emit_norm else out[0]


def _proj_headnorm_kernel(h_ref, w_ref, g_ref, o_ref, *, scale):
    acc = jnp.dot(h_ref[...], w_ref[...], preferred_element_type=F32)
    g = g_ref[...] * scale
    for s in range(acc.shape[1] // HEAD_DIM):
        sl = slice(s * HEAD_DIM, (s + 1) * HEAD_DIM)
        o_ref[:, sl] = _rms(acc[:, sl], g).astype(o_ref.dtype)


def _proj_plain_kernel(h_ref, w_ref, o_ref):
    o_ref[...] = jnp.dot(h_ref[...], w_ref[...], preferred_element_type=F32).astype(o_ref.dtype)


def _proj_glu_kernel(h_ref, wa_ref, wg_ref, o_ref):
    h = h_ref[...]
    a = jnp.dot(h, wa_ref[...], preferred_element_type=F32)
    g = jnp.dot(h, wg_ref[...], preferred_element_type=F32)
    o_ref[...] = a * jax.nn.sigmoid(g)


def _proj(kind, h, w_in, col0, tm, out_dtype, gvec=None, scale=1.0, col1=None):
    m = h.shape[0]
    tn = D_MODEL // 2 if kind == "glu" else D_MODEL
    hspec = pl.BlockSpec((tm, D_MODEL), lambda i, j: (i, 0))
    ospec = pl.BlockSpec((tm, tn), lambda i, j: (i, j))

    def wspec(c0):
        return pl.BlockSpec((D_MODEL, tn), lambda i, j: (0, c0 // tn + j))

    if kind == "headnorm":
        body = functools.partial(_proj_headnorm_kernel, scale=scale)
        in_specs = [hspec, wspec(col0), pl.BlockSpec((1, HEAD_DIM), lambda i, j: (0, 0))]
        args = (h, w_in, gvec)
    elif kind == "glu":
        body = _proj_glu_kernel
        in_specs = [hspec, wspec(col0), wspec(col1)]
        args = (h, w_in, w_in)
    else:
        body = _proj_plain_kernel
        in_specs = [hspec, wspec(col0)]
        args = (h, w_in)
    return pl.pallas_call(
        body, grid=(m // tm, D_MODEL // tn), in_specs=in_specs, out_specs=ospec,
        out_shape=jax.ShapeDtypeStruct((m, D_MODEL), out_dtype),
        compiler_params=_params("parallel", "arbitrary"),
        name="proj_" + kind,
    )(*args)


def _bias_chain(d, tab):
    val = tab(0)
    for t in range(1, N_BUCKETS):
        val = jnp.where(d >= BUCKET_THR[t], tab(t), val)
    return val


def _prompt_bias_kernel(rb_ref, o_ref):
    h = pl.program_id(0)
    shape = (MOBA_BLOCK, MOBA_BLOCK)
    d = lax.broadcasted_iota(jnp.int32, shape, 1) - lax.broadcasted_iota(jnp.int32, shape, 0)
    tab = lambda t: rb_ref[t, h]
    own = _bias_chain(jnp.maximum(d, 0), tab)
    o_ref[0, 0] = jnp.where(d >= 0, own, NEG)
    o_ref[0, 1] = _bias_chain(d + MOBA_BLOCK, tab)
    o_ref[0, 2] = jnp.full(shape, rb_ref[N_BUCKETS - 1, h], F32)


def _prompt_bias(rel_bias):
    return pl.pallas_call(
        _prompt_bias_kernel, grid=(N_HEADS,),
        in_specs=[pl.BlockSpec(memory_space=pltpu.SMEM)],
        out_specs=pl.BlockSpec((1, 3, MOBA_BLOCK, MOBA_BLOCK), lambda h: (h, 0, 0, 0)),
        out_shape=jax.ShapeDtypeStruct((N_HEADS, 3, MOBA_BLOCK, MOBA_BLOCK), F32),
        compiler_params=_params("arbitrary"),
        name="prompt_bias",
    )(rel_bias)


def _sample_bias_kernel(rbt_ref, o_ref):
    shape = (PAGE_SIZE, N_HEADS, HEAD_DIM)
    tab = lambda t: rbt_ref[:, t:t + 1][None]
    d = PAGE_SIZE - lax.broadcasted_iota(jnp.int32, shape, 0)
    o_ref[0] = jnp.broadcast_to(tab(N_BUCKETS - 1), shape)
    o_ref[1] = jnp.broadcast_to(_bias_chain(d, tab), shape)


def _sample_bias(rel_bias_t):
    return pl.pallas_call(
        _sample_bias_kernel,
        out_shape=jax.ShapeDtypeStruct((2, PAGE_SIZE, N_HEADS, HEAD_DIM), F32),
        name="sample_bias",
    )(rel_bias_t)


def _topk_select(gates, k_sel):
    n = len(gates)
    if n <= k_sel:
        return [None] * n
    sel = []
    for j in range(n):
        cnt = jnp.zeros(gates[j].shape, F32)
        for j2 in range(n):
            if j2 == j:
                continue
            ahead = (gates[j2] >= gates[j]) if j2 < j else (gates[j2] > gates[j])
            cnt = cnt + jnp.where(ahead, 1.0, 0.0)
        sel.append(cnt < k_sel)
    return sel


def _prompt_attn_kernel(q_ref, k_ref, v_ref, bias_ref, o_ref, kb_scr, vt_scr, s_scr2, p_scr2):
    t = q_ref.shape[1]
    nb = t // MOBA_BLOCK
    kb_scr[...] = k_ref[0].astype(BF16)
    vt_scr[...] = v_ref[0].T.astype(BF16)
    for i in range(nb):
        s_scr, p_scr = s_scr2.at[i % 2], p_scr2.at[i % 2]
        nk = (i + 1) * MOBA_BLOCK
        qi = q_ref[0, i * MOBA_BLOCK:(i + 1) * MOBA_BLOCK, :]
        s_scr[0:nk, :] = lax.dot_general(kb_scr[0:nk, :], qi, (((1,), (1,)), ((), ())),
                                         preferred_element_type=F32)
        blk = lambda j: slice(j * MOBA_BLOCK, (j + 1) * MOBA_BLOCK)
        gates = [jnp.sum(s_scr[blk(j), :], axis=0, keepdims=True) for j in range(i)]
        sel = _topk_select(gates, MOBA_TOPK)
        m = jnp.full((1, MOBA_BLOCK), NEG, F32)
        for j in range(i + 1):
            kind = 0 if j == i else (1 if j == i - 1 else 2)
            tile = s_scr[blk(j), :] + bias_ref[0, kind]
            if j < i and sel[j] is not None:
                tile = tile + jnp.where(sel[j], 0.0, NEG)
            s_scr[blk(j), :] = tile
            m = jnp.maximum(m, jnp.max(tile, axis=0, keepdims=True))
        l = jnp.zeros((1, MOBA_BLOCK), F32)
        for j in range(i + 1):
            p = jnp.exp(s_scr[blk(j), :] - m)
            l = l + jnp.sum(p, axis=0, keepdims=True)
            p_scr[blk(j), :] = p.astype(BF16)
        ot = jnp.dot(vt_scr[:, 0:nk], p_scr[0:nk, :], preferred_element_type=F32)
        o_ref[0, i * MOBA_BLOCK:(i + 1) * MOBA_BLOCK, :] = (ot / l).T.astype(o_ref.dtype)


def _prompt_attn(q, k, v, bias):
    b, t, _ = q.shape
    spec = pl.BlockSpec((1, t, HEAD_DIM), lambda bi, h: (bi, 0, h))
    return pl.pallas_call(
        _prompt_attn_kernel, grid=(b, N_HEADS),
        in_specs=[spec, spec, spec,
                  pl.BlockSpec((1, 3, MOBA_BLOCK, MOBA_BLOCK), lambda bi, h: (h, 0, 0, 0))],
        out_specs=spec,
        out_shape=jax.ShapeDtypeStruct((b, t, N_HEADS * HEAD_DIM), BF16),
        scratch_shapes=[pltpu.VMEM((t, HEAD_DIM), BF16), pltpu.VMEM((HEAD_DIM, t), BF16),
                        pltpu.VMEM((2, t, MOBA_BLOCK), F32), pltpu.VMEM((2, t, MOBA_BLOCK), BF16)],
        compiler_params=_params("parallel", "arbitrary"),
        name="prompt_attn",
    )(q, k, v, bias)


SAMPLE_PAGES_PER_STEP = 4


def _sample_attn_kernel(pt_ref, q_ref, kn_ref, vn_ref, *refs):
    del pt_ref
    pps = SAMPLE_PAGES_PER_STEP
    k_refs, v_refs = refs[:pps], refs[pps:2 * pps]
    bias_ref, rbt_ref, o_ref, m_scr, l_scr, g_scr, o_scr = refs[2 * pps:]
    step = pl.program_id(1)
    n_pages = o_scr.shape[0]
    hd = (N_HEADS, HEAD_DIM)
    q = q_ref[0]

    for r in range(pps):
        page = step * pps + r
        s = jnp.sum(k_refs[r][...] * q[None], axis=-1, keepdims=True)
        logits = s + bias_ref[jnp.where(page == n_pages - 1, 1, 0)]
        m = jnp.max(logits, axis=0)
        e = jnp.exp(logits - m[None])
        g_scr[page] = jnp.broadcast_to(jnp.sum(s, axis=0), hd)
        m_scr[page] = m
        l_scr[page] = jnp.sum(e, axis=0)
        o_scr[page] = jnp.sum(e * v_refs[r][...], axis=0)

    @pl.when(step == pl.num_programs(1) - 1)
    def _():
        pages_per_block = MOBA_BLOCK // PAGE_SIZE
        n_blocks = n_pages // pages_per_block
        gates = [sum(g_scr[j * pages_per_block + r] for r in range(pages_per_block)) for j in range(n_blocks)]
        sel = _topk_select(gates, MOBA_TOPK)
        s_own = jnp.sum(q * kn_ref[0], axis=-1, keepdims=True) + rbt_ref[:, 0:1]
        m_tot = jnp.broadcast_to(s_own, hd)
        for j in range(n_blocks):
            for r in range(pages_per_block):
                mp = m_scr[j * pages_per_block + r]
                if sel[j] is not None:
                    mp = jnp.where(sel[j], mp, NEG)
                m_tot = jnp.maximum(m_tot, mp)
        w_own = jnp.exp(s_own - m_tot)
        l_tot = w_own
        acc = w_own * vn_ref[0]
        for j in range(n_blocks):
            for r in range(pages_per_block):
                pg = j * pages_per_block + r
                w = jnp.exp(m_scr[pg] - m_tot)
                if sel[j] is not None:
                    w = jnp.where(sel[j], w, 0.0)
                l_tot = l_tot + w * l_scr[pg]
                acc = acc + w * o_scr[pg]
        o_ref[0] = acc / l_tot


def _sample_attn(page_table, q, k_new, v_new, cache_k, cache_v, layer, bias, rel_bias_t):
    s, n_pages = page_table.shape
    pps = SAMPLE_PAGES_PER_STEP
    assert n_pages % pps == 0 and n_pages % (MOBA_BLOCK // PAGE_SIZE) == 0
    row = pl.BlockSpec((1, N_HEADS, HEAD_DIM), lambda si, st, pt: (si, 0, 0))

    def page(r):
        return pl.BlockSpec((None, None, PAGE_SIZE, N_HEADS, HEAD_DIM),
                            lambda si, st, pt: (layer, pt[si, st * pps + r], 0, 0, 0))

    stat = pltpu.VMEM((n_pages, N_HEADS, HEAD_DIM), F32)
    grid_spec = pltpu.PrefetchScalarGridSpec(
        num_scalar_prefetch=1, grid=(s, n_pages // pps),
        in_specs=[row, row, row] + [page(r) for r in range(pps)] * 2
        + [pl.BlockSpec((2, PAGE_SIZE, N_HEADS, HEAD_DIM), lambda si, st, pt: (0, 0, 0, 0)),
           pl.BlockSpec((N_HEADS, N_BUCKETS), lambda si, st, pt: (0, 0))],
        out_specs=row,
        scratch_shapes=[stat, stat, stat, stat],
    )
    return pl.pallas_call(
        _sample_attn_kernel, grid_spec=grid_spec,
        out_shape=jax.ShapeDtypeStruct((s, N_HEADS, HEAD_DIM), F32),
        compiler_params=_params("parallel", "arbitrary"),
        name="sample_attn",
    )(page_table, q, k_new, v_new, *([cache_k] * pps), *([cache_v] * pps), bias, rel_bias_t)


def _ln_swish(y, g, b):
    mu = jnp.mean(y, axis=-1, keepdims=True)
    yc = y - mu
    var = jnp.mean(yc * yc, axis=-1, keepdims=True)
    z = yc * lax.rsqrt(var + EPS) * g + b
    return z * jax.nn.sigmoid(z)


def _conv_prompt_kernel(cur_ref, halo_ref, w_ref, b_ref, g_ref, beta_ref, o_ref, xx_scr, pb_scr, y_scr):
    tt = cur_ref.shape[1]
    first = pl.program_id(1) == 0
    xx_scr[0:CONV_HALO, :] = jnp.where(first, 0.0, halo_ref[0])
    xx_scr[CONV_HALO:CONV_HALO + tt, :] = cur_ref[0]
    xx_scr[CONV_HALO + tt:, :] = jnp.zeros((SUBLANES, xx_scr.shape[1]), F32)
    lead = CONV_HALO - (CONV_K - 1)

    def body(cc, carry):
        cols = pl.ds(pl.multiple_of(cc * LANES, LANES), LANES)
        y = b_ref[:, cols]
        for b in range(SUBLANES):
            acc = None
            for a in range((lead + CONV_K - 1) // SUBLANES + 1):
                j = SUBLANES * a + b - lead
                if 0 <= j < CONV_K:
                    term = w_ref[j:j + 1, cols] * xx_scr[SUBLANES * a:SUBLANES * a + tt + SUBLANES, cols]
                    acc = term if acc is None else acc + term
            pb_scr[b] = acc
            y = y + pb_scr[b, b:b + tt, :]
        y_scr[:, cols] = y
        return carry

    lax.fori_loop(0, cur_ref.shape[2] // LANES, body, 0)
    o_ref[0] = _ln_swish(y_scr[...], g_ref[...], beta_ref[...]).astype(o_ref.dtype)


def _conv_prompt(u, w, b, g, beta, tt=256):
    bsz, t, c = u.shape
    vec = pl.BlockSpec((1, c), lambda bi, i: (0, 0))
    halo_blocks = tt // CONV_HALO
    return pl.pallas_call(
        _conv_prompt_kernel, grid=(bsz, t // tt),
        in_specs=[pl.BlockSpec((1, tt, c), lambda bi, i: (bi, i, 0)),
                  pl.BlockSpec((1, CONV_HALO, c), lambda bi, i: (bi, jnp.maximum(i * halo_blocks - 1, 0), 0)),
                  pl.BlockSpec((CONV_K, c), lambda bi, i: (0, 0)), vec, vec, vec],
        out_specs=pl.BlockSpec((1, tt, c), lambda bi, i: (bi, i, 0)),
        out_shape=jax.ShapeDtypeStruct((bsz, t, c), BF16),
        scratch_shapes=[pltpu.VMEM((CONV_HALO + tt + SUBLANES, c), F32),
                        pltpu.VMEM((SUBLANES, tt + SUBLANES, LANES), F32), pltpu.VMEM((tt, c), F32)],
        compiler_params=_params("parallel", "arbitrary"),
        name="conv_prompt",
    )(u, u, w, b, g, beta)


def _conv_sample_kernel(st_ref, u_ref, w_ref, b_ref, g_ref, beta_ref, y_ref, ns_ref):
    st = st_ref[...]
    u = u_ref[...]
    hist = CONV_K - 1
    y = jnp.sum(st * w_ref[0:hist, :][None], axis=1, keepdims=True) + u * w_ref[hist:CONV_K, :][None]
    y = y + b_ref[...][None]
    y_ref[...] = _ln_swish(y, g_ref[...][None], beta_ref[...][None])
    ns_ref[:, 0:hist - 1, :] = st_ref[:, 1:hist, :]
    ns_ref[:, hist - 1:hist, :] = u


def _conv_sample(state, u, w, b, g, beta, sb=16):
    s, hist, c = state.shape
    vec = pl.BlockSpec((1, c), lambda i: (0, 0))
    return pl.pallas_call(
        _conv_sample_kernel, grid=(s // sb,),
        in_specs=[pl.BlockSpec((sb, hist, c), lambda i: (i, 0, 0)),
                  pl.BlockSpec((sb, 1, c), lambda i: (i, 0, 0)),
                  pl.BlockSpec((CONV_K, c), lambda i: (0, 0)), vec, vec, vec],
        out_specs=[pl.BlockSpec((sb, 1, c), lambda i: (i, 0, 0)),
                   pl.BlockSpec((sb, hist, c), lambda i: (i, 0, 0))],
        out_shape=[jax.ShapeDtypeStruct((s, 1, c), F32), jax.ShapeDtypeStruct((s, hist, c), F32)],
        compiler_params=_params("parallel"),
        name="conv_sample",
    )(state, u, w, b, g, beta)


def _mix_kernel(x_ref, att_ref, yc_ref, h_ref, woa_ref, woc_ref, wga_ref, wgb_ref, wo_ref, o_ref):
    c = pl.program_id(1)

    @pl.when(c == 0)
    def _():
        o_ref[...] = x_ref[...]

    h = h_ref[...]
    a_out = jnp.dot(att_ref[...], woa_ref[...], preferred_element_type=F32)
    c_out = jnp.dot(yc_ref[...], woc_ref[...], preferred_element_type=F32)
    ga = jnp.dot(h, wga_ref[...], preferred_element_type=F32)
    gb = jnp.dot(h, wgb_ref[...], preferred_element_type=F32)
    merged = (jax.nn.sigmoid(ga) * a_out + jax.nn.sigmoid(gb) * c_out).astype(BF16)
    o_ref[...] += jnp.dot(merged, wo_ref[...], preferred_element_type=F32)


def _mix(x, att, yc, h, w_out_att, w_out_conv, w_in, ga_col0, gb_col0, w_out, tm, tn=512):
    m = x.shape[0]
    row = pl.BlockSpec((tm, D_MODEL), lambda i, c: (i, 0))

    def wcol(c0):
        return pl.BlockSpec((D_MODEL, tn), lambda i, c: (0, c0 // tn + c))

    return pl.pallas_call(
        _mix_kernel, grid=(m // tm, D_MODEL // tn),
        in_specs=[row, row, row, row, wcol(0), wcol(0), wcol(ga_col0), wcol(gb_col0),
                  pl.BlockSpec((tn, D_MODEL), lambda i, c: (c, 0))],
        out_specs=row,
        out_shape=jax.ShapeDtypeStruct((m, D_MODEL), F32),
        compiler_params=_params("parallel", "arbitrary"),
        name="mix",
    )(x, att, yc, h, w_out_att, w_out_conv, w_in, w_in, w_out)


def _layer(x, tm, w, attend_fn, conv_fn, q_dtype):
    x1, h = _ffn(x, w["ffn1_norm"], w["ffn1_w_gate"], w["ffn1_w_up"], w["ffn1_w_down"], w["mix_norm"], tm)
    q = _proj("headnorm", h, w["w_in"], 0, tm, q_dtype, gvec=w["q_norm"], scale=SCALE)
    k = _proj("headnorm", h, w["w_in"], D_MODEL, tm, F32, gvec=w["k_norm"])
    v = _proj("plain", h, w["w_in"], 2 * D_MODEL, tm, F32)
    u = _proj("glu", h, w["w_in"], 3 * D_MODEL, tm, F32, col1=4 * D_MODEL)
    att = attend_fn(q, k, v)
    yc, conv_state = conv_fn(u)
    x2 = _mix(x1, att, yc, h, w["w_out_att"], w["w_out_conv"], w["w_in"], 5 * D_MODEL, 6 * D_MODEL,
              w["w_out"], tm)
    y = _ffn(x2, w["ffn2_norm"], w["ffn2_w_gate"], w["ffn2_w_up"], w["ffn2_w_down"], None, tm)
    return y, k, v, conv_state


def kernel(x_prompt, x_sample, cache_k, cache_v, state_conv, page_table, ffn1_norm, ffn1_w_gate, ffn1_w_up,
           ffn1_w_down, mix_norm, w_in, q_norm, k_norm, rel_bias, conv_dw_w, conv_dw_b, conv_ln_g, conv_ln_b,
           w_out_att, w_out_conv, w_out, ffn2_norm, ffn2_w_gate, ffn2_w_up, ffn2_w_down):
    depth = ffn1_norm.shape[0]
    assert depth == 1 and x_sample.shape[1] == 1
    bsz, seq, d = x_prompt.shape
    n_s = x_sample.shape[0]
    n_pages = page_table.shape[1]
    hist = CONV_K - 1
    l = 0
    w = {
        "ffn1_norm": ffn1_norm[l][None], "mix_norm": mix_norm[l][None], "ffn2_norm": ffn2_norm[l][None],
        "q_norm": q_norm[l][None], "k_norm": k_norm[l][None],
        "ffn1_w_gate": ffn1_w_gate[l].astype(BF16), "ffn1_w_up": ffn1_w_up[l].astype(BF16),
        "ffn1_w_down": ffn1_w_down[l].astype(BF16), "w_in": w_in[l].astype(BF16),
        "w_out_att": w_out_att[l].astype(BF16), "w_out_conv": w_out_conv[l].astype(BF16),
        "w_out": w_out[l].astype(BF16),
        "ffn2_w_gate": ffn2_w_gate[l].astype(BF16), "ffn2_w_up": ffn2_w_up[l].astype(BF16),
        "ffn2_w_down": ffn2_w_down[l].astype(BF16),
    }
    dw_w, dw_b = conv_dw_w[l], conv_dw_b[l][None]
    ln_g, ln_b = conv_ln_g[l][None], conv_ln_b[l][None]

    bias_p = _prompt_bias(rel_bias)

    def attend_prompt(q, k, v):
        shp = (bsz, seq, d)
        return _prompt_attn(q.reshape(shp), k.reshape(shp), v.reshape(shp), bias_p).reshape(bsz * seq, d)

    def conv_prompt(u):
        u3 = u.reshape(bsz, seq, d)
        yc = _conv_prompt(u3, dw_w, dw_b, ln_g, ln_b).reshape(bsz * seq, d)
        return yc, u3[:, seq - hist:, :]

    yp, kp, vp, cp = _layer(x_prompt.reshape(bsz * seq, d), 512, w, attend_prompt, conv_prompt, BF16)

    assert n_pages * PAGE_SIZE >= MAX_DISTANCE + PAGE_SIZE
    rel_bias_t = rel_bias.T
    bias_s = _sample_bias(rel_bias_t)

    def attend_sample(q, k, v):
        shp = (n_s, N_HEADS, HEAD_DIM)
        att = _sample_attn(page_table, q.reshape(shp), k.reshape(shp), v.reshape(shp), cache_k, cache_v, l,
                           bias_s, rel_bias_t)
        return att.reshape(n_s, d).astype(BF16)

    def conv_sample(u):
        yc, ns = _conv_sample(state_conv[l], u.reshape(n_s, 1, d), dw_w, dw_b, ln_g, ln_b)
        return yc.reshape(n_s, d).astype(BF16), ns

    ys, ks, vs, cs = _layer(x_sample.reshape(n_s, d), n_s, w, attend_sample, conv_sample, F32)

    heads = (N_HEADS, HEAD_DIM)
    return (yp.reshape(bsz, seq, d), ys.reshape(n_s, 1, d),
            kp.reshape(1, bsz, seq, *heads), vp.reshape(1, bsz, seq, *heads), cp[None],
            ks.reshape(1, n_s, 1, *heads), vs.reshape(1, n_s, 1, *heads), cs[None])
```

```python
import functools
import math

import numpy as np
import jax
import jax.numpy as jnp
from jax import lax
from jax.experimental import pallas as pl
from jax.experimental.pallas import tpu as pltpu

D_MODEL = 2048
N_HEADS = 16
HEAD_DIM = 128
D_FF = 5632
CONV_K = 31
MOBA_BLOCK = 256
MOBA_TOPK = 3
N_BUCKETS = 32
MAX_DISTANCE = 128
PAGE_SIZE = 128
EPS = 1e-6
NEG = -1e30
SCALE = HEAD_DIM ** -0.5
LOG2E = math.log2(math.e)
QK_SCALE = SCALE * LOG2E

F32 = jnp.float32
BF16 = jnp.bfloat16

VMEM_LIMIT_BYTES = 56 * 1024 * 1024
LANES = 128
SUBLANES = 8
BF16_ROWS = 16
CONV_HALO = 32


def _params(*sem):
    return pltpu.CompilerParams(dimension_semantics=sem, vmem_limit_bytes=VMEM_LIMIT_BYTES)


def _t5_bucket_np(n, dtype):
    n = np.maximum(n, 0)
    max_exact = N_BUCKETS // 2
    ratio = np.log(np.maximum(n, 1).astype(dtype) / dtype(max_exact)) / dtype(math.log(MAX_DISTANCE / max_exact))
    large = max_exact + (ratio * dtype(N_BUCKETS - max_exact)).astype(np.int32)
    large = np.minimum(large, N_BUCKETS - 1)
    return np.where(n < max_exact, n, large)


def _bucket_thresholds():
    d = np.arange(0, 8192)
    b32 = _t5_bucket_np(d, np.float32)
    b64 = _t5_bucket_np(d, np.float64)
    assert np.array_equal(b32, b64) and np.all(np.diff(b32) >= 0)
    assert np.all(b32[MAX_DISTANCE:] == N_BUCKETS - 1)
    return [int(np.argmax(b32 >= t)) for t in range(N_BUCKETS)]


BUCKET_THR = _bucket_thresholds()


def _rms(x, g):
    ms = jnp.mean(x * x, axis=-1, keepdims=True)
    return x * lax.rsqrt(ms + EPS) * g


def _ffn_kernel(*refs, emit_norm):
    if emit_norm:
        x_ref, g_ref, wg_ref, wu_ref, wd_ref, gn_ref, o_ref, hn_ref, h_scr = refs
    else:
        x_ref, g_ref, wg_ref, wu_ref, wd_ref, o_ref, h_scr = refs
    f = pl.program_id(1)
    last = pl.num_programs(1) - 1

    def hidden_chunk(h):
        gate = jnp.dot(h, wg_ref[...], preferred_element_type=F32)
        up = jnp.dot(h, wu_ref[...], preferred_element_type=F32)
        hid = (gate * jax.nn.sigmoid(gate) * up).astype(BF16)
        return jnp.dot(hid, wd_ref[...], preferred_element_type=F32)

    @pl.when(f == 0)
    def _():
        h = _rms(x_ref[...], g_ref[...]).astype(BF16)
        h_scr[...] = h
        o_ref[...] = hidden_chunk(h)

    @pl.when(jnp.logical_and(f > 0, f < last))
    def _():
        o_ref[...] += hidden_chunk(h_scr[...])

    @pl.when(f == last)
    def _():
        y = x_ref[...] + 0.5 * (o_ref[...] + hidden_chunk(h_scr[...]))
        o_ref[...] = y
        if emit_norm:
            hn_ref[...] = _rms(y, gn_ref[...]).astype(BF16)


def _ffn(x, g, wg, wu, wd, gn, tm, tf=512):
    m = x.shape[0]
    assert D_FF // tf >= 2
    emit_norm = gn is not None
    row = pl.BlockSpec((tm, D_MODEL), lambda i, f: (i, 0))
    vec = pl.BlockSpec((1, D_MODEL), lambda i, f: (0, 0))
    in_specs = [row, vec,
                pl.BlockSpec((D_MODEL, tf), lambda i, f: (0, f)),
                pl.BlockSpec((D_MODEL, tf), lambda i, f: (0, f)),
                pl.BlockSpec((tf, D_MODEL), lambda i, f: (f, 0))]
    args = [x, g, wg, wu, wd]
    out_shape = [jax.ShapeDtypeStruct((m, D_MODEL), F32)]
    out_specs = [row]
    if emit_norm:
        in_specs.append(vec)
        args.append(gn)
        out_shape.append(jax.ShapeDtypeStruct((m, D_MODEL), BF16))
        out_specs.append(row)
    out = pl.pallas_call(
        functools.partial(_ffn_kernel, emit_norm=emit_norm),
        grid=(m // tm, D_FF // tf),
        in_specs=in_specs, out_specs=out_specs, out_shape=out_shape,
        scratch_shapes=[pltpu.VMEM((tm, D_MODEL), BF16)],
        compiler_params=_params("parallel", "arbitrary"),
        name="ffn",
    )(*args)
    return out if emit_norm else out[0]


def _proj_headnorm_kernel(h_ref, w_ref, g_ref, o_ref, *, scale):
    acc = jnp.dot(h_ref[...], w_ref[...], preferred_element_type=F32)
    g = g_ref[...] * scale
    for s in range(acc.shape[1] // HEAD_DIM):
        sl = slice(s * HEAD_DIM, (s + 1) * HEAD_DIM)
        o_ref[:, sl] = _rms(acc[:, sl], g).astype(o_ref.dtype)


def _proj_plain_kernel(h_ref, w_ref, o_ref):
    o_ref[...] = jnp.dot(h_ref[...], w_ref[...], preferred_element_type=F32).astype(o_ref.dtype)


def _proj_glu_kernel(h_ref, wa_ref, wg_ref, o_ref):
    h = h_ref[...]
    a = jnp.dot(h, wa_ref[...], preferred_element_type=F32)
    g = jnp.dot(h, wg_ref[...], preferred_element_type=F32)
    o_ref[...] = a * jax.nn.sigmoid(g)


def _proj(kind, h, w_in, col0, tm, out_dtype, gvec=None, scale=1.0, col1=None):
    m = h.shape[0]
    tn = D_MODEL // 2 if kind == "glu" else D_MODEL
    hspec = pl.BlockSpec((tm, D_MODEL), lambda i, j: (i, 0))
    ospec = pl.BlockSpec((tm, tn), lambda i, j: (i, j))

    def wspec(c0):
        return pl.BlockSpec((D_MODEL, tn), lambda i, j: (0, c0 // tn + j))

    if kind == "headnorm":
        body = functools.partial(_proj_headnorm_kernel, scale=scale)
        in_specs = [hspec, wspec(col0), pl.BlockSpec((1, HEAD_DIM), lambda i, j: (0, 0))]
        args = (h, w_in, gvec)
    elif kind == "glu":
        body = _proj_glu_kernel
        in_specs = [hspec, wspec(col0), wspec(col1)]
        args = (h, w_in, w_in)
    else:
        body = _proj_plain_kernel
        in_specs = [hspec, wspec(col0)]
        args = (h, w_in)
    return pl.pallas_call(
        body, grid=(m // tm, D_MODEL // tn), in_specs=in_specs, out_specs=ospec,
        out_shape=jax.ShapeDtypeStruct((m, D_MODEL), out_dtype),
        compiler_params=_params("parallel", "arbitrary"),
        name="proj_" + kind,
    )(*args)


def _bias_chain(d, tab):
    val = tab(0)
    for t in range(1, N_BUCKETS):
        val = jnp.where(d >= BUCKET_THR[t], tab(t), val)
    return val


def _prompt_bias_kernel(rb_ref, o_ref):
    h = pl.program_id(0)
    shape = (MOBA_BLOCK, MOBA_BLOCK)
    d = lax.broadcasted_iota(jnp.int32, shape, 1) - lax.broadcasted_iota(jnp.int32, shape, 0)
    tab = lambda t: rb_ref[t, h] * LOG2E
    own = _bias_chain(jnp.maximum(d, 0), tab)
    o_ref[0, 0] = jnp.where(d >= 0, own, NEG)
    o_ref[0, 1] = _bias_chain(d + MOBA_BLOCK, tab)
    o_ref[0, 2] = jnp.full(shape, tab(N_BUCKETS - 1), F32)


def _prompt_bias(rel_bias):
    return pl.pallas_call(
        _prompt_bias_kernel, grid=(N_HEADS,),
        in_specs=[pl.BlockSpec(memory_space=pltpu.SMEM)],
        out_specs=pl.BlockSpec((1, 3, MOBA_BLOCK, MOBA_BLOCK), lambda h: (h, 0, 0, 0)),
        out_shape=jax.ShapeDtypeStruct((N_HEADS, 3, MOBA_BLOCK, MOBA_BLOCK), F32),
        compiler_params=_params("arbitrary"),
        name="prompt_bias",
    )(rel_bias)


def _sample_bias_kernel(rbt_ref, o_ref):
    shape = (PAGE_SIZE, N_HEADS, HEAD_DIM)
    tab = lambda t: rbt_ref[:, t:t + 1][None] * LOG2E
    d = PAGE_SIZE - lax.broadcasted_iota(jnp.int32, shape, 0)
    o_ref[0] = jnp.broadcast_to(tab(N_BUCKETS - 1), shape)
    o_ref[1] = jnp.broadcast_to(_bias_chain(d, tab), shape)


def _sample_bias(rel_bias_t):
    return pl.pallas_call(
        _sample_bias_kernel,
        out_shape=jax.ShapeDtypeStruct((2, PAGE_SIZE, N_HEADS, HEAD_DIM), F32),
        name="sample_bias",
    )(rel_bias_t)


def _topk_select(gates, k_sel):
    n = len(gates)
    if n <= k_sel:
        return [None] * n
    sel = []
    for j in range(n):
        cnt = jnp.zeros(gates[j].shape, F32)
        for j2 in range(n):
            if j2 == j:
                continue
            ahead = (gates[j2] >= gates[j]) if j2 < j else (gates[j2] > gates[j])
            cnt = cnt + jnp.where(ahead, 1.0, 0.0)
        sel.append(cnt < k_sel)
    return sel


def _prompt_attn_kernel(q_ref, k_ref, v_ref, bias_ref, o_ref, kb_scr, vt_scr, s_scr, p_scr):
    t = q_ref.shape[1]
    nb = t // MOBA_BLOCK
    nt = (((1,), (1,)), ((), ()))
    blk = lambda j: slice(j * MOBA_BLOCK, (j + 1) * MOBA_BLOCK)
    k = k_ref[0]
    kb_scr[...] = k.astype(BF16)
    vt_scr[0:HEAD_DIM, :] = v_ref[0].T.astype(BF16)
    vt_scr[HEAD_DIM:, :] = jnp.ones((BF16_ROWS, t), BF16)
    ksum = jnp.sum(k.reshape(nb, MOBA_BLOCK, HEAD_DIM), axis=1)
    ks_hi = ksum.astype(BF16)
    ks_lo = (ksum - ks_hi.astype(F32)).astype(BF16)
    far_row = bias_ref[0, 2][0:1, :]
    qs = [q_ref[0, blk(i), :] for i in range(nb)]
    masks = {}
    for i in range(MOBA_TOPK + 1, nb):
        g = (lax.dot_general(ks_hi, qs[i], nt, preferred_element_type=F32)
             + lax.dot_general(ks_lo, qs[i], nt, preferred_element_type=F32))
        sel = _topk_select([g[j:j + 1, :] for j in range(i)], MOBA_TOPK)
        masks[i] = [jnp.where(sj, 0.0, NEG) for sj in sel]

    def biased_scores(i):
        nk = (i + 1) * MOBA_BLOCK
        sv = lax.dot_general(kb_scr[0:nk, :], qs[i], nt, preferred_element_type=F32)
        m = None
        for j in range(i + 1):
            tile = sv[blk(j), :]
            mask = masks[i][j] if (i in masks and j < i) else None
            if j == i:
                tile = tile + bias_ref[0, 0]
            elif j == i - 1:
                tile = tile + bias_ref[0, 1]
                if mask is not None:
                    tile = tile + mask
            else:
                tile = tile + (far_row if mask is None else far_row + mask)
            s_scr[i % 2, blk(j), :] = tile
            cm = jnp.max(tile, axis=0, keepdims=True)
            m = cm if m is None else jnp.maximum(m, cm)
        return m

    m_next = biased_scores(0)
    for i in range(nb):
        m = m_next
        if i + 1 < nb:
            m_next = biased_scores(i + 1)
        nk = (i + 1) * MOBA_BLOCK
        for j in range(i + 1):
            p_scr[i % 2, blk(j), :] = jnp.exp2(s_scr[i % 2, blk(j), :] - m).astype(BF16)
        acc = jnp.dot(vt_scr[:, 0:nk], p_scr[i % 2, 0:nk, :], preferred_element_type=F32)
        out = acc[0:HEAD_DIM, :] / acc[HEAD_DIM:HEAD_DIM + 1, :]
        o_ref[0, blk(i), :] = out.T.astype(o_ref.dtype)


def _prompt_attn(q, k, v, bias):
    b, t, _ = q.shape
    spec = pl.BlockSpec((1, t, HEAD_DIM), lambda bi, h: (bi, 0, h))
    return pl.pallas_call(
        _prompt_attn_kernel, grid=(b, N_HEADS),
        in_specs=[spec, spec, spec,
                  pl.BlockSpec((1, 3, MOBA_BLOCK, MOBA_BLOCK), lambda bi, h: (h, 0, 0, 0))],
        out_specs=spec,
        out_shape=jax.ShapeDtypeStruct((b, t, N_HEADS * HEAD_DIM), BF16),
        scratch_shapes=[pltpu.VMEM((t, HEAD_DIM), BF16), pltpu.VMEM((HEAD_DIM + BF16_ROWS, t), BF16),
                        pltpu.VMEM((2, t, MOBA_BLOCK), F32), pltpu.VMEM((2, t, MOBA_BLOCK), BF16)],
        compiler_params=_params("parallel", "arbitrary"),
        name="prompt_attn",
    )(q, k, v, bias)


SAMPLE_PAGES_PER_STEP = 8


def _sample_attn_kernel(pt_ref, q_ref, kn_ref, vn_ref, *refs):
    del pt_ref
    pps = SAMPLE_PAGES_PER_STEP
    k_refs, v_refs = refs[:pps], refs[pps:2 * pps]
    bias_ref, rbt_ref, o_ref, m_scr, l_scr, g_scr, o_scr = refs[2 * pps:]
    step = pl.program_id(1)
    n_pages = o_scr.shape[0]
    hd = (N_HEADS, HEAD_DIM)
    q = q_ref[0]

    last_step = step == pl.num_programs(1) - 1
    far = rbt_ref[:, N_BUCKETS - 1:N_BUCKETS] * LOG2E
    for r in range(pps):
        page = step * pps + r
        s = jnp.sum(k_refs[r][...] * q[None], axis=-1, keepdims=True)
        if r < pps - 1:
            logits = s
            m = jnp.max(s, axis=0)
            m_scr[page] = jnp.broadcast_to(m + far, hd)
        else:
            logits = s + bias_ref[jnp.where(last_step, 1, 0)]
            m = jnp.max(logits, axis=0)
            m_scr[page] = m
        e = jnp.exp2(logits - m[None])
        g_scr[page] = jnp.broadcast_to(jnp.sum(s, axis=0), hd)
        l_scr[page] = jnp.broadcast_to(jnp.sum(e, axis=0), hd)
        o_scr[page] = jnp.sum(e * v_refs[r][...], axis=0)

    @pl.when(last_step)
    def _():
        pages_per_block = MOBA_BLOCK // PAGE_SIZE
        n_blocks = n_pages // pages_per_block
        gates = [sum(g_scr[j * pages_per_block + r] for r in range(pages_per_block)) for j in range(n_blocks)]
        sel = _topk_select(gates, MOBA_TOPK)
        s_own = jnp.sum(q * kn_ref[0], axis=-1, keepdims=True) + rbt_ref[:, 0:1] * LOG2E
        m_tot = jnp.broadcast_to(s_own, hd)
        for j in range(n_blocks):
            for r in range(pages_per_block):
                mp = m_scr[j * pages_per_block + r]
                if sel[j] is not None:
                    mp = jnp.where(sel[j], mp, NEG)
                m_tot = jnp.maximum(m_tot, mp)
        w_own = jnp.exp2(s_own - m_tot)
        l_tot = w_own
        acc = w_own * vn_ref[0]
        for j in range(n_blocks):
            for r in range(pages_per_block):
                pg = j * pages_per_block + r
                w = jnp.exp2(m_scr[pg] - m_tot)
                if sel[j] is not None:
                    w = jnp.where(sel[j], w, 0.0)
                l_tot = l_tot + w * l_scr[pg]
                acc = acc + w * o_scr[pg]
        o_ref[0] = acc / l_tot


def _sample_attn(page_table, q, k_new, v_new, cache_k, cache_v, layer, bias, rel_bias_t):
    s, n_pages = page_table.shape
    pps = SAMPLE_PAGES_PER_STEP
    assert n_pages % pps == 0 and n_pages % (MOBA_BLOCK // PAGE_SIZE) == 0
    row = pl.BlockSpec((1, N_HEADS, HEAD_DIM), lambda si, st, pt: (si, 0, 0))

    def page(r):
        return pl.BlockSpec((None, None, PAGE_SIZE, N_HEADS, HEAD_DIM),
                            lambda si, st, pt: (layer, pt[si, st * pps + r], 0, 0, 0))

    stat = pltpu.VMEM((n_pages, N_HEADS, HEAD_DIM), F32)
    grid_spec = pltpu.PrefetchScalarGridSpec(
        num_scalar_prefetch=1, grid=(s, n_pages // pps),
        in_specs=[row, row, row] + [page(r) for r in range(pps)] * 2
        + [pl.BlockSpec((2, PAGE_SIZE, N_HEADS, HEAD_DIM), lambda si, st, pt: (0, 0, 0, 0)),
           pl.BlockSpec((N_HEADS, N_BUCKETS), lambda si, st, pt: (0, 0))],
        out_specs=row,
        scratch_shapes=[stat, stat, stat, stat],
    )
    return pl.pallas_call(
        _sample_attn_kernel, grid_spec=grid_spec,
        out_shape=jax.ShapeDtypeStruct((s, N_HEADS, HEAD_DIM), F32),
        compiler_params=_params("parallel", "arbitrary"),
        name="sample_attn",
    )(page_table, q, k_new, v_new, *([cache_k] * pps), *([cache_v] * pps), bias, rel_bias_t)


def _ln_swish(y, g, b):
    mu = jnp.mean(y, axis=-1, keepdims=True)
    yc = y - mu
    var = jnp.mean(yc * yc, axis=-1, keepdims=True)
    z = yc * lax.rsqrt(var + EPS) * g + b
    return z * jax.nn.sigmoid(z)


def _conv_prompt_kernel(cur_ref, halo_ref, w_ref, b_ref, g_ref, beta_ref, o_ref, xx_scr, pb_scr, y_scr):
    tt = cur_ref.shape[1]
    first = pl.program_id(1) == 0
    xx_scr[0:CONV_HALO, :] = jnp.where(first, 0.0, halo_ref[0])
    xx_scr[CONV_HALO:CONV_HALO + tt, :] = cur_ref[0]
    xx_scr[CONV_HALO + tt:, :] = jnp.zeros((SUBLANES, xx_scr.shape[1]), F32)
    lead = CONV_HALO - (CONV_K - 1)

    def body(cc, carry):
        cols = pl.ds(pl.multiple_of(cc * LANES, LANES), LANES)
        y = b_ref[:, cols]
        for b in range(SUBLANES):
            acc = None
            for a in range((lead + CONV_K - 1) // SUBLANES + 1):
                j = SUBLANES * a + b - lead
                if 0 <= j < CONV_K:
                    term = w_ref[j:j + 1, cols] * xx_scr[SUBLANES * a:SUBLANES * a + tt + SUBLANES, cols]
                    acc = term if acc is None else acc + term
            pb_scr[b] = acc
            y = y + pb_scr[b, b:b + tt, :]
        y_scr[:, cols] = y
        return carry

    lax.fori_loop(0, cur_ref.shape[2] // LANES, body, 0)
    o_ref[0] = _ln_swish(y_scr[...], g_ref[...], beta_ref[...]).astype(o_ref.dtype)


def _conv_prompt(u, w, b, g, beta, tt=256):
    bsz, t, c = u.shape
    vec = pl.BlockSpec((1, c), lambda bi, i: (0, 0))
    halo_blocks = tt // CONV_HALO
    return pl.pallas_call(
        _conv_prompt_kernel, grid=(bsz, t // tt),
        in_specs=[pl.BlockSpec((1, tt, c), lambda bi, i: (bi, i, 0)),
                  pl.BlockSpec((1, CONV_HALO, c), lambda bi, i: (bi, jnp.maximum(i * halo_blocks - 1, 0), 0)),
                  pl.BlockSpec((CONV_K, c), lambda bi, i: (0, 0)), vec, vec, vec],
        out_specs=pl.BlockSpec((1, tt, c), lambda bi, i: (bi, i, 0)),
        out_shape=jax.ShapeDtypeStruct((bsz, t, c), BF16),
        scratch_shapes=[pltpu.VMEM((CONV_HALO + tt + SUBLANES, c), F32),
                        pltpu.VMEM((SUBLANES, tt + SUBLANES, LANES), F32), pltpu.VMEM((tt, c), F32)],
        compiler_params=_params("parallel", "arbitrary"),
        name="conv_prompt",
    )(u, u, w, b, g, beta)


def _conv_sample_kernel(st_ref, u_ref, w_ref, b_ref, g_ref, beta_ref, y_ref, ns_ref):
    st = st_ref[...]
    u = u_ref[...]
    hist = CONV_K - 1
    y = jnp.sum(st * w_ref[0:hist, :][None], axis=1, keepdims=True) + u * w_ref[hist:CONV_K, :][None]
    y = y + b_ref[...][None]
    y_ref[...] = _ln_swish(y, g_ref[...][None], beta_ref[...][None])
    ns_ref[:, 0:hist - 1, :] = st_ref[:, 1:hist, :]
    ns_ref[:, hist - 1:hist, :] = u


def _conv_sample(state, u, w, b, g, beta, sb=16):
    s, hist, c = state.shape
    vec = pl.BlockSpec((1, c), lambda i: (0, 0))
    return pl.pallas_call(
        _conv_sample_kernel, grid=(s // sb,),
        in_specs=[pl.BlockSpec((sb, hist, c), lambda i: (i, 0, 0)),
                  pl.BlockSpec((sb, 1, c), lambda i: (i, 0, 0)),
                  pl.BlockSpec((CONV_K, c), lambda i: (0, 0)), vec, vec, vec],
        out_specs=[pl.BlockSpec((sb, 1, c), lambda i: (i, 0, 0)),
                   pl.BlockSpec((sb, hist, c), lambda i: (i, 0, 0))],
        out_shape=[jax.ShapeDtypeStruct((s, 1, c), F32), jax.ShapeDtypeStruct((s, hist, c), F32)],
        compiler_params=_params("parallel"),
        name="conv_sample",
    )(state, u, w, b, g, beta)


def _mix_kernel(x_ref, att_ref, yc_ref, h_ref, woa_ref, woc_ref, wga_ref, wgb_ref, wo_ref, o_ref):
    c = pl.program_id(1)

    def merged_chunk():
        h = h_ref[...]
        a_out = jnp.dot(att_ref[...], woa_ref[...], preferred_element_type=F32)
        c_out = jnp.dot(yc_ref[...], woc_ref[...], preferred_element_type=F32)
        ga = jnp.dot(h, wga_ref[...], preferred_element_type=F32)
        gb = jnp.dot(h, wgb_ref[...], preferred_element_type=F32)
        merged = (jax.nn.sigmoid(ga) * a_out + jax.nn.sigmoid(gb) * c_out).astype(BF16)
        return jnp.dot(merged, wo_ref[...], preferred_element_type=F32)

    @pl.when(c == 0)
    def _():
        o_ref[...] = x_ref[...] + merged_chunk()

    @pl.when(c > 0)
    def _():
        o_ref[...] += merged_chunk()


def _mix(x, att, yc, h, w_out_att, w_out_conv, w_in, ga_col0, gb_col0, w_out, tm, tn=512):
    m = x.shape[0]
    row = pl.BlockSpec((tm, D_MODEL), lambda i, c: (i, 0))

    def wcol(c0):
        return pl.BlockSpec((D_MODEL, tn), lambda i, c: (0, c0 // tn + c))

    return pl.pallas_call(
        _mix_kernel, grid=(m // tm, D_MODEL // tn),
        in_specs=[row, row, row, row, wcol(0), wcol(0), wcol(ga_col0), wcol(gb_col0),
                  pl.BlockSpec((tn, D_MODEL), lambda i, c: (c, 0))],
        out_specs=row,
        out_shape=jax.ShapeDtypeStruct((m, D_MODEL), F32),
        compiler_params=_params("parallel", "arbitrary"),
        name="mix",
    )(x, att, yc, h, w_out_att, w_out_conv, w_in, w_in, w_out)


def _layer(x, tm, w, attend_fn, conv_fn, q_dtype):
    x1, h = _ffn(x, w["ffn1_norm"], w["ffn1_w_gate"], w["ffn1_w_up"], w["ffn1_w_down"], w["mix_norm"], tm)
    q = _proj("headnorm", h, w["w_in"], 0, tm, q_dtype, gvec=w["q_norm"], scale=QK_SCALE)
    k = _proj("headnorm", h, w["w_in"], D_MODEL, tm, F32, gvec=w["k_norm"])
    v = _proj("plain", h, w["w_in"], 2 * D_MODEL, tm, F32)
    u = _proj("glu", h, w["w_in"], 3 * D_MODEL, tm, F32, col1=4 * D_MODEL)
    att = attend_fn(q, k, v)
    yc, conv_state = conv_fn(u)
    x2 = _mix(x1, att, yc, h, w["w_out_att"], w["w_out_conv"], w["w_in"], 5 * D_MODEL, 6 * D_MODEL,
              w["w_out"], tm)
    y = _ffn(x2, w["ffn2_norm"], w["ffn2_w_gate"], w["ffn2_w_up"], w["ffn2_w_down"], None, tm)
    return y, k, v, conv_state


def kernel(x_prompt, x_sample, cache_k, cache_v, state_conv, page_table, ffn1_norm, ffn1_w_gate, ffn1_w_up,
           ffn1_w_down, mix_norm, w_in, q_norm, k_norm, rel_bias, conv_dw_w, conv_dw_b, conv_ln_g, conv_ln_b,
           w_out_att, w_out_conv, w_out, ffn2_norm, ffn2_w_gate, ffn2_w_up, ffn2_w_down):
    depth = ffn1_norm.shape[0]
    assert depth == 1 and x_sample.shape[1] == 1
    bsz, seq, d = x_prompt.shape
    n_s = x_sample.shape[0]
    n_pages = page_table.shape[1]
    hist = CONV_K - 1
    l = 0
    w = {
        "ffn1_norm": ffn1_norm[l][None], "mix_norm": mix_norm[l][None], "ffn2_norm": ffn2_norm[l][None],
        "q_norm": q_norm[l][None], "k_norm": k_norm[l][None],
        "ffn1_w_gate": ffn1_w_gate[l].astype(BF16), "ffn1_w_up": ffn1_w_up[l].astype(BF16),
        "ffn1_w_down": ffn1_w_down[l].astype(BF16), "w_in": w_in[l].astype(BF16),
        "w_out_att": w_out_att[l].astype(BF16), "w_out_conv": w_out_conv[l].astype(BF16),
        "w_out": w_out[l].astype(BF16),
        "ffn2_w_gate": ffn2_w_gate[l].astype(BF16), "ffn2_w_up": ffn2_w_up[l].astype(BF16),
        "ffn2_w_down": ffn2_w_down[l].astype(BF16),
    }
    dw_w, dw_b = conv_dw_w[l], conv_dw_b[l][None]
    ln_g, ln_b = conv_ln_g[l][None], conv_ln_b[l][None]

    bias_p = _prompt_bias(rel_bias)

    def attend_prompt(q, k, v):
        shp = (bsz, seq, d)
        return _prompt_attn(q.reshape(shp), k.reshape(shp), v.reshape(shp), bias_p).reshape(bsz * seq, d)

    def conv_prompt(u):
        u3 = u.reshape(bsz, seq, d)
        yc = _conv_prompt(u3, dw_w, dw_b, ln_g, ln_b).reshape(bsz * seq, d)
        return yc, u3[:, seq - hist:, :]

    yp, kp, vp, cp = _layer(x_prompt.reshape(bsz * seq, d), 512, w, attend_prompt, conv_prompt, BF16)

    assert n_pages * PAGE_SIZE >= MAX_DISTANCE + PAGE_SIZE
    rel_bias_t = rel_bias.T
    bias_s = _sample_bias(rel_bias_t)

    def attend_sample(q, k, v):
        shp = (n_s, N_HEADS, HEAD_DIM)
        att = _sample_attn(page_table, q.reshape(shp), k.reshape(shp), v.reshape(shp), cache_k, cache_v, l,
                           bias_s, rel_bias_t)
        return att.reshape(n_s, d).astype(BF16)

    def conv_sample(u):
        yc, ns = _conv_sample(state_conv[l], u.reshape(n_s, 1, d), dw_w, dw_b, ln_g, ln_b)
        return yc.reshape(n_s, d).astype(BF16), ns

    ys, ks, vs, cs = _layer(x_sample.reshape(n_s, d), n_s, w, attend_sample, conv_sample, F32)

    heads = (N_HEADS, HEAD_DIM)
    return (yp.reshape(bsz, seq, d), ys.reshape(n_s, 1, d),
            kp.reshape(1, bsz, seq, *heads), vp.reshape(1, bsz, seq, *heads), cp[None],
            ks.reshape(1, n_s, 1, *heads), vs.reshape(1, n_s, 1, *heads), cs[None])
```

```python
import functools
import math

import numpy as np
import jax
import jax.numpy as jnp
from jax import lax
from jax.experimental import pallas as pl
from jax.experimental.pallas import tpu as pltpu

D_MODEL = 2048
N_HEADS = 16
HEAD_DIM = 128
D_FF = 5632
CONV_K = 31
MOBA_BLOCK = 256
MOBA_TOPK = 3
N_BUCKETS = 32
MAX_DISTANCE = 128
PAGE_SIZE = 128
EPS = 1e-6
NEG = -1e30
SCALE = HEAD_DIM ** -0.5
LOG2E = math.log2(math.e)
QK_SCALE = SCALE * LOG2E

F32 = jnp.float32
BF16 = jnp.bfloat16

VMEM_LIMIT_BYTES = 56 * 1024 * 1024
LANES = 128
SUBLANES = 8
BF16_ROWS = 16
CONV_HALO = 32
FFN_TF = 512
MIX_TN = 512
ATTN_LOOKAHEAD = 2


def _params(*sem):
    return pltpu.CompilerParams(dimension_semantics=sem, vmem_limit_bytes=VMEM_LIMIT_BYTES)


def _t5_bucket_np(n, dtype):
    n = np.maximum(n, 0)
    max_exact = N_BUCKETS // 2
    ratio = np.log(np.maximum(n, 1).astype(dtype) / dtype(max_exact)) / dtype(math.log(MAX_DISTANCE / max_exact))
    large = max_exact + (ratio * dtype(N_BUCKETS - max_exact)).astype(np.int32)
    large = np.minimum(large, N_BUCKETS - 1)
    return np.where(n < max_exact, n, large)


def _bucket_thresholds():
    d = np.arange(0, 8192)
    b32 = _t5_bucket_np(d, np.float32)
    b64 = _t5_bucket_np(d, np.float64)
    assert np.array_equal(b32, b64) and np.all(np.diff(b32) >= 0)
    assert np.all(b32[MAX_DISTANCE:] == N_BUCKETS - 1)
    return [int(np.argmax(b32 >= t)) for t in range(N_BUCKETS)]


BUCKET_THR = _bucket_thresholds()


def _col_tiles(w, tn):
    k, n = w.shape
    return w.astype(BF16).reshape(k, n // tn, tn).transpose(1, 0, 2)


def _rms(x, g):
    ms = jnp.mean(x * x, axis=-1, keepdims=True)
    return x * lax.rsqrt(ms + EPS) * g


def _ffn_kernel(*refs, emit_norm):
    if emit_norm:
        x_ref, g_ref, wg_ref, wu_ref, wd_ref, gn_ref, o_ref, hn_ref, h_scr = refs
    else:
        x_ref, g_ref, wg_ref, wu_ref, wd_ref, o_ref, h_scr = refs
    f = pl.program_id(1)
    last = pl.num_programs(1) - 1

    def hidden_chunk(h):
        gate = jnp.dot(h, wg_ref[...], preferred_element_type=F32)
        up = jnp.dot(h, wu_ref[...], preferred_element_type=F32)
        hid = (gate * jax.nn.sigmoid(gate) * up).astype(BF16)
        return jnp.dot(hid, wd_ref[...], preferred_element_type=F32)

    @pl.when(f == 0)
    def _():
        h = _rms(x_ref[...], g_ref[...]).astype(BF16)
        h_scr[...] = h
        o_ref[...] = hidden_chunk(h)

    @pl.when(jnp.logical_and(f > 0, f < last))
    def _():
        o_ref[...] += hidden_chunk(h_scr[...])

    @pl.when(f == last)
    def _():
        y = x_ref[...] + 0.5 * (o_ref[...] + hidden_chunk(h_scr[...]))
        o_ref[...] = y
        if emit_norm:
            hn_ref[...] = _rms(y, gn_ref[...]).astype(BF16)


def _ffn(x, g, wg, wu, wd, gn, tm):
    m = x.shape[0]
    tf = FFN_TF
    assert D_FF // tf >= 2
    emit_norm = gn is not None
    row = pl.BlockSpec((tm, D_MODEL), lambda i, f: (i, 0))
    vec = pl.BlockSpec((1, D_MODEL), lambda i, f: (0, 0))
    in_specs = [row, vec,
                pl.BlockSpec((None, D_MODEL, tf), lambda i, f: (f, 0, 0)),
                pl.BlockSpec((None, D_MODEL, tf), lambda i, f: (f, 0, 0)),
                pl.BlockSpec((tf, D_MODEL), lambda i, f: (f, 0))]
    args = [x, g, wg, wu, wd]
    out_shape = [jax.ShapeDtypeStruct((m, D_MODEL), F32)]
    out_specs = [row]
    if emit_norm:
        in_specs.append(vec)
        args.append(gn)
        out_shape.append(jax.ShapeDtypeStruct((m, D_MODEL), BF16))
        out_specs.append(row)
    out = pl.pallas_call(
        functools.partial(_ffn_kernel, emit_norm=emit_norm),
        grid=(m // tm, D_FF // tf),
        in_specs=in_specs, out_specs=out_specs, out_shape=out_shape,
        scratch_shapes=[pltpu.VMEM((tm, D_MODEL), BF16)],
        compiler_params=_params("parallel", "arbitrary"),
        name="ffn",
    )(*args)
    return out if emit_norm else out[0]


def _proj_headnorm_kernel(h_ref, w_ref, g_ref, o_ref, *, scale):
    acc = jnp.dot(h_ref[...], w_ref[...], preferred_element_type=F32)
    g = g_ref[...] * scale
    for s in range(acc.shape[1] // HEAD_DIM):
        sl = slice(s * HEAD_DIM, (s + 1) * HEAD_DIM)
        o_ref[:, sl] = _rms(acc[:, sl], g).astype(o_ref.dtype)


def _proj_plain_kernel(h_ref, w_ref, o_ref):
    o_ref[...] = jnp.dot(h_ref[...], w_ref[...], preferred_element_type=F32).astype(o_ref.dtype)


def _proj_glu_kernel(h_ref, wa_ref, wg_ref, o_ref):
    h = h_ref[...]
    a = jnp.dot(h, wa_ref[...], preferred_element_type=F32)
    g = jnp.dot(h, wg_ref[...], preferred_element_type=F32)
    o_ref[...] = a * jax.nn.sigmoid(g)


def _proj(kind, h, w_in, col0, tm, out_dtype, gvec=None, scale=1.0, col1=None):
    m = h.shape[0]
    tn = D_MODEL // 2 if kind == "glu" else D_MODEL
    hspec = pl.BlockSpec((tm, D_MODEL), lambda i, j: (i, 0))
    ospec = pl.BlockSpec((tm, tn), lambda i, j: (i, j))

    def wspec(c0):
        return pl.BlockSpec((D_MODEL, tn), lambda i, j: (0, c0 // tn + j))

    if kind == "headnorm":
        body = functools.partial(_proj_headnorm_kernel, scale=scale)
        in_specs = [hspec, wspec(col0), pl.BlockSpec((1, HEAD_DIM), lambda i, j: (0, 0))]
        args = (h, w_in, gvec)
    elif kind == "glu":
        body = _proj_glu_kernel
        in_specs = [hspec, wspec(col0), wspec(col1)]
        args = (h, w_in, w_in)
    else:
        body = _proj_plain_kernel
        in_specs = [hspec, wspec(col0)]
        args = (h, w_in)
    return pl.pallas_call(
        body, grid=(m // tm, D_MODEL // tn), in_specs=in_specs, out_specs=ospec,
        out_shape=jax.ShapeDtypeStruct((m, D_MODEL), out_dtype),
        compiler_params=_params("parallel", "arbitrary"),
        name="proj_" + kind,
    )(*args)


def _bias_chain(d, tab):
    val = tab(0)
    for t in range(1, N_BUCKETS):
        val = jnp.where(d >= BUCKET_THR[t], tab(t), val)
    return val


def _prompt_bias_kernel(rb_ref, o_ref):
    h = pl.program_id(0)
    shape = (MOBA_BLOCK, MOBA_BLOCK)
    d = lax.broadcasted_iota(jnp.int32, shape, 1) - lax.broadcasted_iota(jnp.int32, shape, 0)
    tab = lambda t: rb_ref[t, h] * LOG2E
    own = _bias_chain(jnp.maximum(d, 0), tab)
    o_ref[0, 0] = jnp.where(d >= 0, own, NEG)
    o_ref[0, 1] = _bias_chain(d + MOBA_BLOCK, tab)
    o_ref[0, 2] = jnp.full(shape, tab(N_BUCKETS - 1), F32)


def _prompt_bias(rel_bias):
    return pl.pallas_call(
        _prompt_bias_kernel, grid=(N_HEADS,),
        in_specs=[pl.BlockSpec(memory_space=pltpu.SMEM)],
        out_specs=pl.BlockSpec((1, 3, MOBA_BLOCK, MOBA_BLOCK), lambda h: (h, 0, 0, 0)),
        out_shape=jax.ShapeDtypeStruct((N_HEADS, 3, MOBA_BLOCK, MOBA_BLOCK), F32),
        compiler_params=_params("arbitrary"),
        name="prompt_bias",
    )(rel_bias)


def _sample_bias_kernel(rbt_ref, o_ref):
    shape = (PAGE_SIZE, N_HEADS, HEAD_DIM)
    tab = lambda t: rbt_ref[:, t:t + 1][None] * LOG2E
    d = PAGE_SIZE - lax.broadcasted_iota(jnp.int32, shape, 0)
    o_ref[0] = jnp.broadcast_to(tab(N_BUCKETS - 1), shape)
    o_ref[1] = jnp.broadcast_to(_bias_chain(d, tab), shape)


def _sample_bias(rel_bias_t):
    return pl.pallas_call(
        _sample_bias_kernel,
        out_shape=jax.ShapeDtypeStruct((2, PAGE_SIZE, N_HEADS, HEAD_DIM), F32),
        name="sample_bias",
    )(rel_bias_t)


def _topk_select(gates, k_sel):
    n = len(gates)
    if n <= k_sel:
        return [None] * n
    sel = []
    for j in range(n):
        cnt = jnp.zeros(gates[j].shape, F32)
        for j2 in range(n):
            if j2 == j:
                continue
            ahead = (gates[j2] >= gates[j]) if j2 < j else (gates[j2] > gates[j])
            cnt = cnt + jnp.where(ahead, 1.0, 0.0)
        sel.append(cnt < k_sel)
    return sel


def _prompt_attn_kernel(q_ref, k_ref, v_ref, bias_ref, o_ref, kb_scr, vt_scr, s_scr, p_scr):
    t = q_ref.shape[1]
    nb = t // MOBA_BLOCK
    nbuf = s_scr.shape[0]
    nt = (((1,), (1,)), ((), ()))
    blk = lambda j: slice(j * MOBA_BLOCK, (j + 1) * MOBA_BLOCK)
    k = k_ref[0]
    kb_scr[...] = k.astype(BF16)
    vt_scr[0:HEAD_DIM, :] = v_ref[0].T.astype(BF16)
    vt_scr[HEAD_DIM:, :] = jnp.ones((BF16_ROWS, t), BF16)
    ksum = jnp.sum(k.reshape(nb, MOBA_BLOCK, HEAD_DIM), axis=1)
    ks_hi = ksum.astype(BF16)
    ks_lo = (ksum - ks_hi.astype(F32)).astype(BF16)
    far_row = bias_ref[0, 2][0:1, :]
    qs = [q_ref[0, blk(i), :] for i in range(nb)]
    masks = {}
    for i in range(MOBA_TOPK + 1, nb):
        g = (lax.dot_general(ks_hi, qs[i], nt, preferred_element_type=F32)
             + lax.dot_general(ks_lo, qs[i], nt, preferred_element_type=F32))
        sel = _topk_select([g[j:j + 1, :] for j in range(i)], MOBA_TOPK)
        masks[i] = [jnp.where(sj, 0.0, NEG) for sj in sel]

    def biased_scores(i):
        nk = (i + 1) * MOBA_BLOCK
        sv = lax.dot_general(kb_scr[0:nk, :], qs[i], nt, preferred_element_type=F32)
        m = None
        for j in range(i + 1):
            tile = sv[blk(j), :]
            mask = masks[i][j] if (i in masks and j < i) else None
            if j == i:
                tile = tile + bias_ref[0, 0]
            elif j == i - 1:
                tile = tile + bias_ref[0, 1]
                if mask is not None:
                    tile = tile + mask
            else:
                tile = tile + (far_row if mask is None else far_row + mask)
            s_scr[i % nbuf, blk(j), :] = tile
            cm = jnp.max(tile, axis=0, keepdims=True)
            m = cm if m is None else jnp.maximum(m, cm)
        return m

    col_max = {i: biased_scores(i) for i in range(min(ATTN_LOOKAHEAD, nb))}
    for i in range(nb):
        if i + ATTN_LOOKAHEAD < nb:
            col_max[i + ATTN_LOOKAHEAD] = biased_scores(i + ATTN_LOOKAHEAD)
        m = col_max.pop(i)
        nk = (i + 1) * MOBA_BLOCK
        for j in range(i + 1):
            p_scr[i % nbuf, blk(j), :] = jnp.exp2(s_scr[i % nbuf, blk(j), :] - m).astype(BF16)
        acc = jnp.dot(vt_scr[:, 0:nk], p_scr[i % nbuf, 0:nk, :], preferred_element_type=F32)
        out = acc[0:HEAD_DIM, :] / acc[HEAD_DIM:HEAD_DIM + 1, :]
        o_ref[0, blk(i), :] = out.T.astype(o_ref.dtype)


def _prompt_attn(q, k, v, bias):
    b, t, _ = q.shape
    spec = pl.BlockSpec((1, t, HEAD_DIM), lambda bi, h: (bi, 0, h))
    return pl.pallas_call(
        _prompt_attn_kernel, grid=(b, N_HEADS),
        in_specs=[spec, spec, spec,
                  pl.BlockSpec((1, 3, MOBA_BLOCK, MOBA_BLOCK), lambda bi, h: (h, 0, 0, 0))],
        out_specs=spec,
        out_shape=jax.ShapeDtypeStruct((b, t, N_HEADS * HEAD_DIM), BF16),
        scratch_shapes=[pltpu.VMEM((t, HEAD_DIM), BF16), pltpu.VMEM((HEAD_DIM + BF16_ROWS, t), BF16),
                        pltpu.VMEM((ATTN_LOOKAHEAD + 1, t, MOBA_BLOCK), F32),
                        pltpu.VMEM((ATTN_LOOKAHEAD + 1, t, MOBA_BLOCK), BF16)],
        compiler_params=_params("parallel", "arbitrary"),
        name="prompt_attn",
    )(q, k, v, bias)


SAMPLE_PAGES_PER_STEP = 8


def _sample_attn_kernel(pt_ref, q_ref, kn_ref, vn_ref, *refs):
    del pt_ref
    pps = SAMPLE_PAGES_PER_STEP
    k_refs, v_refs = refs[:pps], refs[pps:2 * pps]
    bias_ref, rbt_ref, o_ref, m_scr, l_scr, g_scr, o_scr = refs[2 * pps:]
    step = pl.program_id(1)
    n_pages = o_scr.shape[0]
    hd = (N_HEADS, HEAD_DIM)
    q = q_ref[0]

    last_step = step == pl.num_programs(1) - 1
    far = rbt_ref[:, N_BUCKETS - 1:N_BUCKETS] * LOG2E
    for r in range(pps):
        page = step * pps + r
        s = jnp.sum(k_refs[r][...] * q[None], axis=-1, keepdims=True)
        if r < pps - 1:
            logits = s
            m = jnp.max(s, axis=0)
            m_scr[page] = jnp.broadcast_to(m + far, hd)
        else:
            logits = s + bias_ref[jnp.where(last_step, 1, 0)]
            m = jnp.max(logits, axis=0)
            m_scr[page] = m
        e = jnp.exp2(logits - m[None])
        g_scr[page] = jnp.broadcast_to(jnp.sum(s, axis=0), hd)
        l_scr[page] = jnp.broadcast_to(jnp.sum(e, axis=0), hd)
        o_scr[page] = jnp.sum(e * v_refs[r][...], axis=0)

    @pl.when(last_step)
    def _():
        pages_per_block = MOBA_BLOCK // PAGE_SIZE
        n_blocks = n_pages // pages_per_block
        gates = [sum(g_scr[j * pages_per_block + r] for r in range(pages_per_block)) for j in range(n_blocks)]
        sel = _topk_select(gates, MOBA_TOPK)
        s_own = jnp.sum(q * kn_ref[0], axis=-1, keepdims=True) + rbt_ref[:, 0:1] * LOG2E
        m_tot = jnp.broadcast_to(s_own, hd)
        for j in range(n_blocks):
            for r in range(pages_per_block):
                mp = m_scr[j * pages_per_block + r]
                if sel[j] is not None:
                    mp = jnp.where(sel[j], mp, NEG)
                m_tot = jnp.maximum(m_tot, mp)
        w_own = jnp.exp2(s_own - m_tot)
        l_tot = w_own
        acc = w_own * vn_ref[0]
        for j in range(n_blocks):
            for r in range(pages_per_block):
                pg = j * pages_per_block + r
                w = jnp.exp2(m_scr[pg] - m_tot)
                if sel[j] is not None:
                    w = jnp.where(sel[j], w, 0.0)
                l_tot = l_tot + w * l_scr[pg]
                acc = acc + w * o_scr[pg]
        o_ref[0] = acc / l_tot


def _sample_attn(page_table, q, k_new, v_new, cache_k, cache_v, layer, bias, rel_bias_t):
    s, n_pages = page_table.shape
    pps = SAMPLE_PAGES_PER_STEP
    assert n_pages % pps == 0 and n_pages % (MOBA_BLOCK // PAGE_SIZE) == 0
    row = pl.BlockSpec((1, N_HEADS, HEAD_DIM), lambda si, st, pt: (si, 0, 0))

    def page(r):
        return pl.BlockSpec((None, None, PAGE_SIZE, N_HEADS, HEAD_DIM),
                            lambda si, st, pt: (layer, pt[si, st * pps + r], 0, 0, 0))

    stat = pltpu.VMEM((n_pages, N_HEADS, HEAD_DIM), F32)
    grid_spec = pltpu.PrefetchScalarGridSpec(
        num_scalar_prefetch=1, grid=(s, n_pages // pps),
        in_specs=[row, row, row] + [page(r) for r in range(pps)] * 2
        + [pl.BlockSpec((2, PAGE_SIZE, N_HEADS, HEAD_DIM), lambda si, st, pt: (0, 0, 0, 0)),
           pl.BlockSpec((N_HEADS, N_BUCKETS), lambda si, st, pt: (0, 0))],
        out_specs=row,
        scratch_shapes=[stat, stat, stat, stat],
    )
    return pl.pallas_call(
        _sample_attn_kernel, grid_spec=grid_spec,
        out_shape=jax.ShapeDtypeStruct((s, N_HEADS, HEAD_DIM), F32),
        compiler_params=_params("parallel", "arbitrary"),
        name="sample_attn",
    )(page_table, q, k_new, v_new, *([cache_k] * pps), *([cache_v] * pps), bias, rel_bias_t)


def _ln_swish(y, g, b):
    mu = jnp.mean(y, axis=-1, keepdims=True)
    yc = y - mu
    var = jnp.mean(yc * yc, axis=-1, keepdims=True)
    z = yc * lax.rsqrt(var + EPS) * g + b
    return z * jax.nn.sigmoid(z)


def _conv_prompt_kernel(cur_ref, halo_ref, w_ref, b_ref, g_ref, beta_ref, o_ref, xx_scr, pb_scr, y_scr):
    tt = cur_ref.shape[1]
    first = pl.program_id(1) == 0
    xx_scr[0:CONV_HALO, :] = jnp.where(first, 0.0, halo_ref[0])
    xx_scr[CONV_HALO:CONV_HALO + tt, :] = cur_ref[0]
    xx_scr[CONV_HALO + tt:, :] = jnp.zeros((SUBLANES, xx_scr.shape[1]), F32)
    lead = CONV_HALO - (CONV_K - 1)

    def body(cc, carry):
        cols = pl.ds(pl.multiple_of(cc * LANES, LANES), LANES)
        y = b_ref[:, cols]
        for b in range(SUBLANES):
            acc = None
            for a in range((lead + CONV_K - 1) // SUBLANES + 1):
                j = SUBLANES * a + b - lead
                if 0 <= j < CONV_K:
                    term = w_ref[j:j + 1, cols] * xx_scr[SUBLANES * a:SUBLANES * a + tt + SUBLANES, cols]
                    acc = term if acc is None else acc + term
            pb_scr[b] = acc
            y = y + pb_scr[b, b:b + tt, :]
        y_scr[:, cols] = y
        return carry

    lax.fori_loop(0, cur_ref.shape[2] // LANES, body, 0)
    o_ref[0] = _ln_swish(y_scr[...], g_ref[...], beta_ref[...]).astype(o_ref.dtype)


def _conv_prompt(u, w, b, g, beta, tt=256):
    bsz, t, c = u.shape
    vec = pl.BlockSpec((1, c), lambda bi, i: (0, 0))
    halo_blocks = tt // CONV_HALO
    return pl.pallas_call(
        _conv_prompt_kernel, grid=(bsz, t // tt),
        in_specs=[pl.BlockSpec((1, tt, c), lambda bi, i: (bi, i, 0)),
                  pl.BlockSpec((1, CONV_HALO, c), lambda bi, i: (bi, jnp.maximum(i * halo_blocks - 1, 0), 0)),
                  pl.BlockSpec((CONV_K, c), lambda bi, i: (0, 0)), vec, vec, vec],
        out_specs=pl.BlockSpec((1, tt, c), lambda bi, i: (bi, i, 0)),
        out_shape=jax.ShapeDtypeStruct((bsz, t, c), BF16),
        scratch_shapes=[pltpu.VMEM((CONV_HALO + tt + SUBLANES, c), F32),
                        pltpu.VMEM((SUBLANES, tt + SUBLANES, LANES), F32), pltpu.VMEM((tt, c), F32)],
        compiler_params=_params("parallel", "arbitrary"),
        name="conv_prompt",
    )(u, u, w, b, g, beta)


def _conv_sample_kernel(st_ref, u_ref, w_ref, b_ref, g_ref, beta_ref, y_ref, ns_ref):
    st = st_ref[...]
    u = u_ref[...]
    hist = CONV_K - 1
    y = jnp.sum(st * w_ref[0:hist, :][None], axis=1, keepdims=True) + u * w_ref[hist:CONV_K, :][None]
    y = y + b_ref[...][None]
    y_ref[...] = _ln_swish(y, g_ref[...][None], beta_ref[...][None])
    ns_ref[:, 0:hist - 1, :] = st_ref[:, 1:hist, :]
    ns_ref[:, hist - 1:hist, :] = u


def _conv_sample(state, u, w, b, g, beta, sb=16):
    s, hist, c = state.shape
    vec = pl.BlockSpec((1, c), lambda i: (0, 0))
    return pl.pallas_call(
        _conv_sample_kernel, grid=(s // sb,),
        in_specs=[pl.BlockSpec((sb, hist, c), lambda i: (i, 0, 0)),
                  pl.BlockSpec((sb, 1, c), lambda i: (i, 0, 0)),
                  pl.BlockSpec((CONV_K, c), lambda i: (0, 0)), vec, vec, vec],
        out_specs=[pl.BlockSpec((sb, 1, c), lambda i: (i, 0, 0)),
                   pl.BlockSpec((sb, hist, c), lambda i: (i, 0, 0))],
        out_shape=[jax.ShapeDtypeStruct((s, 1, c), F32), jax.ShapeDtypeStruct((s, hist, c), F32)],
        compiler_params=_params("parallel"),
        name="conv_sample",
    )(state, u, w, b, g, beta)


def _mix_kernel(x_ref, att_ref, yc_ref, h_ref, woa_ref, woc_ref, wga_ref, wgb_ref, wo_ref, o_ref):
    c = pl.program_id(1)

    def merged_chunk():
        h = h_ref[...]
        a_out = jnp.dot(att_ref[...], woa_ref[...], preferred_element_type=F32)
        c_out = jnp.dot(yc_ref[...], woc_ref[...], preferred_element_type=F32)
        ga = jnp.dot(h, wga_ref[...], preferred_element_type=F32)
        gb = jnp.dot(h, wgb_ref[...], preferred_element_type=F32)
        merged = (jax.nn.sigmoid(ga) * a_out + jax.nn.sigmoid(gb) * c_out).astype(BF16)
        return jnp.dot(merged, wo_ref[...], preferred_element_type=F32)

    @pl.when(c == 0)
    def _():
        o_ref[...] = x_ref[...] + merged_chunk()

    @pl.when(c > 0)
    def _():
        o_ref[...] += merged_chunk()


def _mix(x, att, yc, h, w_out_att, w_out_conv, w_gates, w_out, tm):
    m = x.shape[0]
    tn = MIX_TN
    nt = D_MODEL // tn
    row = pl.BlockSpec((tm, D_MODEL), lambda i, c: (i, 0))

    def wcol(t0):
        return pl.BlockSpec((None, D_MODEL, tn), lambda i, c: (t0 + c, 0, 0))

    return pl.pallas_call(
        _mix_kernel, grid=(m // tm, nt),
        in_specs=[row, row, row, row, wcol(0), wcol(0), wcol(0), wcol(nt),
                  pl.BlockSpec((tn, D_MODEL), lambda i, c: (c, 0))],
        out_specs=row,
        out_shape=jax.ShapeDtypeStruct((m, D_MODEL), F32),
        compiler_params=_params("parallel", "arbitrary"),
        name="mix",
    )(x, att, yc, h, w_out_att, w_out_conv, w_gates, w_gates, w_out)


def _layer(x, tm, w, attend_fn, conv_fn, q_dtype):
    x1, h = _ffn(x, w["ffn1_norm"], w["ffn1_w_gate"], w["ffn1_w_up"], w["ffn1_w_down"], w["mix_norm"], tm)
    q = _proj("headnorm", h, w["w_in"], 0, tm, q_dtype, gvec=w["q_norm"], scale=QK_SCALE)
    k = _proj("headnorm", h, w["w_in"], D_MODEL, tm, F32, gvec=w["k_norm"])
    v = _proj("plain", h, w["w_in"], 2 * D_MODEL, tm, F32)
    u = _proj("glu", h, w["w_in"], 3 * D_MODEL, tm, F32, col1=4 * D_MODEL)
    att = attend_fn(q, k, v)
    yc, conv_state = conv_fn(u)
    x2 = _mix(x1, att, yc, h, w["w_out_att"], w["w_out_conv"], w["w_gates"], w["w_out"], tm)
    y = _ffn(x2, w["ffn2_norm"], w["ffn2_w_gate"], w["ffn2_w_up"], w["ffn2_w_down"], None, tm)
    return y, k, v, conv_state


def kernel(x_prompt, x_sample, cache_k, cache_v, state_conv, page_table, ffn1_norm, ffn1_w_gate, ffn1_w_up,
           ffn1_w_down, mix_norm, w_in, q_norm, k_norm, rel_bias, conv_dw_w, conv_dw_b, conv_ln_g, conv_ln_b,
           w_out_att, w_out_conv, w_out, ffn2_norm, ffn2_w_gate, ffn2_w_up, ffn2_w_down):
    depth = ffn1_norm.shape[0]
    assert depth == 1 and x_sample.shape[1] == 1
    bsz, seq, d = x_prompt.shape
    n_s = x_sample.shape[0]
    n_pages = page_table.shape[1]
    hist = CONV_K - 1
    l = 0
    n_proj = 5 * D_MODEL
    w = {
        "ffn1_norm": ffn1_norm[l][None], "mix_norm": mix_norm[l][None], "ffn2_norm": ffn2_norm[l][None],
        "q_norm": q_norm[l][None], "k_norm": k_norm[l][None],
        "ffn1_w_gate": _col_tiles(ffn1_w_gate[l], FFN_TF), "ffn1_w_up": _col_tiles(ffn1_w_up[l], FFN_TF),
        "ffn1_w_down": ffn1_w_down[l].astype(BF16), "w_in": w_in[l][:, :n_proj].astype(BF16),
        "w_gates": _col_tiles(w_in[l][:, n_proj:], MIX_TN),
        "w_out_att": _col_tiles(w_out_att[l], MIX_TN), "w_out_conv": _col_tiles(w_out_conv[l], MIX_TN),
        "w_out": w_out[l].astype(BF16),
        "ffn2_w_gate": _col_tiles(ffn2_w_gate[l], FFN_TF), "ffn2_w_up": _col_tiles(ffn2_w_up[l], FFN_TF),
        "ffn2_w_down": ffn2_w_down[l].astype(BF16),
    }
    dw_w, dw_b = conv_dw_w[l], conv_dw_b[l][None]
    ln_g, ln_b = conv_ln_g[l][None], conv_ln_b[l][None]

    bias_p = _prompt_bias(rel_bias)

    def attend_prompt(q, k, v):
        shp = (bsz, seq, d)
        return _prompt_attn(q.reshape(shp), k.reshape(shp), v.reshape(shp), bias_p).reshape(bsz * seq, d)

    def conv_prompt(u):
        u3 = u.reshape(bsz, seq, d)
        yc = _conv_prompt(u3, dw_w, dw_b, ln_g, ln_b).reshape(bsz * seq, d)
        return yc, u3[:, seq - hist:, :]

    yp, kp, vp, cp = _layer(x_prompt.reshape(bsz * seq, d), 512, w, attend_prompt, conv_prompt, BF16)

    assert n_pages * PAGE_SIZE >= MAX_DISTANCE + PAGE_SIZE
    rel_bias_t = rel_bias.T
    bias_s = _sample_bias(rel_bias_t)

    def attend_sample(q, k, v):
        shp = (n_s, N_HEADS, HEAD_DIM)
        att = _sample_attn(page_table, q.reshape(shp), k.reshape(shp), v.reshape(shp), cache_k, cache_v, l,
                           bias_s, rel_bias_t)
        return att.reshape(n_s, d).astype(BF16)

    def conv_sample(u):
        yc, ns = _conv_sample(state_conv[l], u.reshape(n_s, 1, d), dw_w, dw_b, ln_g, ln_b)
        return yc.reshape(n_s, d).astype(BF16), ns

    ys, ks, vs, cs = _layer(x_sample.reshape(n_s, d), n_s, w, attend_sample, conv_sample, F32)

    heads = (N_HEADS, HEAD_DIM)
    return (yp.reshape(bsz, seq, d), ys.reshape(n_s, 1, d),
            kp.reshape(1, bsz, seq, *heads), vp.reshape(1, bsz, seq, *heads), cp[None],
            ks.reshape(1, n_s, 1, *heads), vs.reshape(1, n_s, 1, *heads), cs[None])
```

```python
import functools
import math

import numpy as np
import jax
import jax.numpy as jnp
from jax import lax
from jax.experimental import pallas as pl
from jax.experimental.pallas import tpu as pltpu

D_MODEL = 2048
N_HEADS = 16
HEAD_DIM = 128
D_FF = 5632
CONV_K = 31
MOBA_BLOCK = 256
MOBA_TOPK = 3
N_BUCKETS = 32
MAX_DISTANCE = 128
PAGE_SIZE = 128
EPS = 1e-6
NEG = -1e30
SCALE = HEAD_DIM ** -0.5
LOG2E = math.log2(math.e)
QK_SCALE = SCALE * LOG2E

F32 = jnp.float32
BF16 = jnp.bfloat16

VMEM_LIMIT_BYTES = 56 * 1024 * 1024
LANES = 128
SUBLANES = 8
BF16_ROWS = 16
CONV_HALO = 32
FFN_TF = 512
MIX_TN = 512
ATTN_LOOKAHEAD = 2


def _params(*sem):
    return pltpu.CompilerParams(dimension_semantics=sem, vmem_limit_bytes=VMEM_LIMIT_BYTES)


def _t5_bucket_np(n, dtype):
    n = np.maximum(n, 0)
    max_exact = N_BUCKETS // 2
    ratio = np.log(np.maximum(n, 1).astype(dtype) / dtype(max_exact)) / dtype(math.log(MAX_DISTANCE / max_exact))
    large = max_exact + (ratio * dtype(N_BUCKETS - max_exact)).astype(np.int32)
    large = np.minimum(large, N_BUCKETS - 1)
    return np.where(n < max_exact, n, large)


def _bucket_thresholds():
    d = np.arange(0, 8192)
    b32 = _t5_bucket_np(d, np.float32)
    b64 = _t5_bucket_np(d, np.float64)
    assert np.array_equal(b32, b64) and np.all(np.diff(b32) >= 0)
    assert np.all(b32[MAX_DISTANCE:] == N_BUCKETS - 1)
    return [int(np.argmax(b32 >= t)) for t in range(N_BUCKETS)]


BUCKET_THR = _bucket_thresholds()


def _rms(x, g):
    ms = jnp.mean(x * x, axis=-1, keepdims=True)
    return x * lax.rsqrt(ms + EPS) * g


def _ffn_kernel(*refs, emit_norm):
    if emit_norm:
        x_ref, g_ref, wg_ref, wu_ref, wd_ref, gn_ref, o_ref, hn_ref, h_scr = refs
    else:
        x_ref, g_ref, wg_ref, wu_ref, wd_ref, o_ref, h_scr = refs
    f = pl.program_id(1)
    last = pl.num_programs(1) - 1

    def hidden_chunk(h):
        gate = jnp.dot(h, wg_ref[...], preferred_element_type=F32)
        up = jnp.dot(h, wu_ref[...], preferred_element_type=F32)
        hid = (gate * jax.nn.sigmoid(gate) * up).astype(BF16)
        return jnp.dot(hid, wd_ref[...], preferred_element_type=F32)

    @pl.when(f == 0)
    def _():
        h = _rms(x_ref[...], g_ref[...]).astype(BF16)
        h_scr[...] = h
        o_ref[...] = hidden_chunk(h)

    @pl.when(jnp.logical_and(f > 0, f < last))
    def _():
        o_ref[...] += hidden_chunk(h_scr[...])

    @pl.when(f == last)
    def _():
        y = x_ref[...] + 0.5 * (o_ref[...] + hidden_chunk(h_scr[...]))
        o_ref[...] = y
        if emit_norm:
            hn_ref[...] = _rms(y, gn_ref[...]).astype(BF16)


def _ffn(x, g, wg, wu, wd, gn, tm):
    m = x.shape[0]
    tf = FFN_TF
    assert D_FF // tf >= 2
    emit_norm = gn is not None
    row = pl.BlockSpec((tm, D_MODEL), lambda i, f: (i, 0))
    vec = pl.BlockSpec((1, D_MODEL), lambda i, f: (0, 0))
    in_specs = [row, vec,
                pl.BlockSpec((D_MODEL, tf), lambda i, f: (0, f)),
                pl.BlockSpec((D_MODEL, tf), lambda i, f: (0, f)),
                pl.BlockSpec((tf, D_MODEL), lambda i, f: (f, 0))]
    args = [x, g, wg, wu, wd]
    out_shape = [jax.ShapeDtypeStruct((m, D_MODEL), F32)]
    out_specs = [row]
    if emit_norm:
        in_specs.append(vec)
        args.append(gn)
        out_shape.append(jax.ShapeDtypeStruct((m, D_MODEL), BF16))
        out_specs.append(row)
    out = pl.pallas_call(
        functools.partial(_ffn_kernel, emit_norm=emit_norm),
        grid=(m // tm, D_FF // tf),
        in_specs=in_specs, out_specs=out_specs, out_shape=out_shape,
        scratch_shapes=[pltpu.VMEM((tm, D_MODEL), BF16)],
        compiler_params=_params("parallel", "arbitrary"),
        name="ffn",
    )(*args)
    return out if emit_norm else out[0]


def _proj_headnorm_kernel(h_ref, w_ref, g_ref, o_ref, *, scale):
    acc = jnp.dot(h_ref[...], w_ref[...], preferred_element_type=F32)
    g = g_ref[...] * scale
    for s in range(acc.shape[1] // HEAD_DIM):
        sl = slice(s * HEAD_DIM, (s + 1) * HEAD_DIM)
        o_ref[:, sl] = _rms(acc[:, sl], g).astype(o_ref.dtype)


def _proj_plain_kernel(h_ref, w_ref, o_ref):
    o_ref[...] = jnp.dot(h_ref[...], w_ref[...], preferred_element_type=F32).astype(o_ref.dtype)


def _proj_glu_kernel(h_ref, wa_ref, wg_ref, o_ref):
    h = h_ref[...]
    a = jnp.dot(h, wa_ref[...], preferred_element_type=F32)
    g = jnp.dot(h, wg_ref[...], preferred_element_type=F32)
    o_ref[...] = a * jax.nn.sigmoid(g)


def _proj(kind, h, w_in, col0, tm, out_dtype, gvec=None, scale=1.0, col1=None):
    m = h.shape[0]
    tn = D_MODEL // 2 if kind == "glu" else D_MODEL
    hspec = pl.BlockSpec((tm, D_MODEL), lambda i, j: (i, 0))
    ospec = pl.BlockSpec((tm, tn), lambda i, j: (i, j))

    def wspec(c0):
        return pl.BlockSpec((D_MODEL, tn), lambda i, j: (0, c0 // tn + j))

    if kind == "headnorm":
        body = functools.partial(_proj_headnorm_kernel, scale=scale)
        in_specs = [hspec, wspec(col0), pl.BlockSpec((1, HEAD_DIM), lambda i, j: (0, 0))]
        args = (h, w_in, gvec)
    elif kind == "glu":
        body = _proj_glu_kernel
        in_specs = [hspec, wspec(col0), wspec(col1)]
        args = (h, w_in, w_in)
    else:
        body = _proj_plain_kernel
        in_specs = [hspec, wspec(col0)]
        args = (h, w_in)
    return pl.pallas_call(
        body, grid=(m // tm, D_MODEL // tn), in_specs=in_specs, out_specs=ospec,
        out_shape=jax.ShapeDtypeStruct((m, D_MODEL), out_dtype),
        compiler_params=_params("parallel", "arbitrary"),
        name="proj_" + kind,
    )(*args)


def _bias_chain(d, tab):
    val = tab(0)
    for t in range(1, N_BUCKETS):
        val = jnp.where(d >= BUCKET_THR[t], tab(t), val)
    return val


def _prompt_bias_kernel(rb_ref, o_ref):
    h = pl.program_id(0)
    shape = (MOBA_BLOCK, MOBA_BLOCK)
    d = lax.broadcasted_iota(jnp.int32, shape, 1) - lax.broadcasted_iota(jnp.int32, shape, 0)
    tab = lambda t: rb_ref[t, h] * LOG2E
    own = _bias_chain(jnp.maximum(d, 0), tab)
    o_ref[0, 0] = jnp.where(d >= 0, own, NEG)
    o_ref[0, 1] = _bias_chain(d + MOBA_BLOCK, tab)
    o_ref[0, 2] = jnp.full(shape, tab(N_BUCKETS - 1), F32)


def _prompt_bias(rel_bias):
    return pl.pallas_call(
        _prompt_bias_kernel, grid=(N_HEADS,),
        in_specs=[pl.BlockSpec(memory_space=pltpu.SMEM)],
        out_specs=pl.BlockSpec((1, 3, MOBA_BLOCK, MOBA_BLOCK), lambda h: (h, 0, 0, 0)),
        out_shape=jax.ShapeDtypeStruct((N_HEADS, 3, MOBA_BLOCK, MOBA_BLOCK), F32),
        compiler_params=_params("arbitrary"),
        name="prompt_bias",
    )(rel_bias)


def _sample_bias_kernel(rbt_ref, o_ref):
    shape = (PAGE_SIZE, N_HEADS, HEAD_DIM)
    tab = lambda t: rbt_ref[:, t:t + 1][None] * LOG2E
    d = PAGE_SIZE - lax.broadcasted_iota(jnp.int32, shape, 0)
    o_ref[0] = jnp.broadcast_to(tab(N_BUCKETS - 1), shape)
    o_ref[1] = jnp.broadcast_to(_bias_chain(d, tab), shape)


def _sample_bias(rel_bias_t):
    return pl.pallas_call(
        _sample_bias_kernel,
        out_shape=jax.ShapeDtypeStruct((2, PAGE_SIZE, N_HEADS, HEAD_DIM), F32),
        name="sample_bias",
    )(rel_bias_t)


def _topk_select(gates, k_sel):
    n = len(gates)
    if n <= k_sel:
        return [None] * n
    sel = []
    for j in range(n):
        cnt = jnp.zeros(gates[j].shape, F32)
        for j2 in range(n):
            if j2 == j:
                continue
            ahead = (gates[j2] >= gates[j]) if j2 < j else (gates[j2] > gates[j])
            cnt = cnt + jnp.where(ahead, 1.0, 0.0)
        sel.append(cnt < k_sel)
    return sel


def _prompt_attn_kernel(q_ref, k_ref, v_ref, bias_ref, o_ref, kb_scr, vt_scr, s_scr, p_scr):
    t = q_ref.shape[1]
    nb = t // MOBA_BLOCK
    nbuf = s_scr.shape[0]
    nt = (((1,), (1,)), ((), ()))
    blk = lambda j: slice(j * MOBA_BLOCK, (j + 1) * MOBA_BLOCK)
    k = k_ref[0]
    kb_scr[...] = k.astype(BF16)
    vt_scr[0:HEAD_DIM, :] = v_ref[0].T.astype(BF16)
    vt_scr[HEAD_DIM:, :] = jnp.ones((BF16_ROWS, t), BF16)
    ksum = jnp.sum(k.reshape(nb, MOBA_BLOCK, HEAD_DIM), axis=1)
    ks_hi = ksum.astype(BF16)
    ks_lo = (ksum - ks_hi.astype(F32)).astype(BF16)
    far_row = bias_ref[0, 2][0:1, :]
    qs = [q_ref[0, blk(i), :] for i in range(nb)]
    masks = {}
    for i in range(MOBA_TOPK + 1, nb):
        g = (lax.dot_general(ks_hi, qs[i], nt, preferred_element_type=F32)
             + lax.dot_general(ks_lo, qs[i], nt, preferred_element_type=F32))
        sel = _topk_select([g[j:j + 1, :] for j in range(i)], MOBA_TOPK)
        masks[i] = [jnp.where(sj, 0.0, NEG) for sj in sel]

    def biased_scores(i):
        nk = (i + 1) * MOBA_BLOCK
        sv = lax.dot_general(kb_scr[0:nk, :], qs[i], nt, preferred_element_type=F32)
        m = None
        for j in range(i + 1):
            tile = sv[blk(j), :]
            mask = masks[i][j] if (i in masks and j < i) else None
            if j == i:
                tile = tile + bias_ref[0, 0]
            elif j == i - 1:
                tile = tile + bias_ref[0, 1]
                if mask is not None:
                    tile = tile + mask
            else:
                tile = tile + (far_row if mask is None else far_row + mask)
            s_scr[i % nbuf, blk(j), :] = tile
            cm = jnp.max(tile, axis=0, keepdims=True)
            m = cm if m is None else jnp.maximum(m, cm)
        return m

    col_max = {i: biased_scores(i) for i in range(min(ATTN_LOOKAHEAD, nb))}
    for i in range(nb):
        if i + ATTN_LOOKAHEAD < nb:
            col_max[i + ATTN_LOOKAHEAD] = biased_scores(i + ATTN_LOOKAHEAD)
        m = col_max.pop(i)
        nk = (i + 1) * MOBA_BLOCK
        for j in range(i + 1):
            p_scr[i % nbuf, blk(j), :] = jnp.exp2(s_scr[i % nbuf, blk(j), :] - m).astype(BF16)
        acc = jnp.dot(vt_scr[:, 0:nk], p_scr[i % nbuf, 0:nk, :], preferred_element_type=F32)
        out = acc[0:HEAD_DIM, :] / acc[HEAD_DIM:HEAD_DIM + 1, :]
        o_ref[0, blk(i), :] = out.T.astype(o_ref.dtype)


def _prompt_attn(q, k, v, bias):
    b, t, _ = q.shape
    spec = pl.BlockSpec((1, t, HEAD_DIM), lambda bi, h: (bi, 0, h))
    return pl.pallas_call(
        _prompt_attn_kernel, grid=(b, N_HEADS),
        in_specs=[spec, spec, spec,
                  pl.BlockSpec((1, 3, MOBA_BLOCK, MOBA_BLOCK), lambda bi, h: (h, 0, 0, 0))],
        out_specs=spec,
        out_shape=jax.ShapeDtypeStruct((b, t, N_HEADS * HEAD_DIM), BF16),
        scratch_shapes=[pltpu.VMEM((t, HEAD_DIM), BF16), pltpu.VMEM((HEAD_DIM + BF16_ROWS, t), BF16),
                        pltpu.VMEM((ATTN_LOOKAHEAD + 1, t, MOBA_BLOCK), F32),
                        pltpu.VMEM((ATTN_LOOKAHEAD + 1, t, MOBA_BLOCK), BF16)],
        compiler_params=_params("parallel", "arbitrary"),
        name="prompt_attn",
    )(q, k, v, bias)


SAMPLE_PAGES_PER_STEP = 8


def _sample_attn_kernel(pt_ref, q_ref, kn_ref, vn_ref, *refs):
    del pt_ref
    pps = SAMPLE_PAGES_PER_STEP
    k_refs, v_refs = refs[:pps], refs[pps:2 * pps]
    bias_ref, rbt_ref, o_ref, m_scr, l_scr, g_scr, o_scr = refs[2 * pps:]
    step = pl.program_id(1)
    n_pages = o_scr.shape[0]
    hd = (N_HEADS, HEAD_DIM)
    q = q_ref[0]

    last_step = step == pl.num_programs(1) - 1
    far = rbt_ref[:, N_BUCKETS - 1:N_BUCKETS] * LOG2E
    for r in range(pps):
        page = step * pps + r
        s = jnp.sum(k_refs[r][...] * q[None], axis=-1, keepdims=True)
        if r < pps - 1:
            logits = s
            m = jnp.max(s, axis=0)
            m_scr[page] = jnp.broadcast_to(m + far, hd)
        else:
            logits = s + bias_ref[jnp.where(last_step, 1, 0)]
            m = jnp.max(logits, axis=0)
            m_scr[page] = m
        e = jnp.exp2(logits - m[None])
        g_scr[page] = jnp.broadcast_to(jnp.sum(s, axis=0), hd)
        l_scr[page] = jnp.broadcast_to(jnp.sum(e, axis=0), hd)
        o_scr[page] = jnp.sum(e * v_refs[r][...], axis=0)

    @pl.when(last_step)
    def _():
        pages_per_block = MOBA_BLOCK // PAGE_SIZE
        n_blocks = n_pages // pages_per_block
        gates = [sum(g_scr[j * pages_per_block + r] for r in range(pages_per_block)) for j in range(n_blocks)]
        sel = _topk_select(gates, MOBA_TOPK)
        s_own = jnp.sum(q * kn_ref[0], axis=-1, keepdims=True) + rbt_ref[:, 0:1] * LOG2E
        m_tot = jnp.broadcast_to(s_own, hd)
        for j in range(n_blocks):
            for r in range(pages_per_block):
                mp = m_scr[j * pages_per_block + r]
                if sel[j] is not None:
                    mp = jnp.where(sel[j], mp, NEG)
                m_tot = jnp.maximum(m_tot, mp)
        w_own = jnp.exp2(s_own - m_tot)
        l_tot = w_own
        acc = w_own * vn_ref[0]
        for j in range(n_blocks):
            for r in range(pages_per_block):
                pg = j * pages_per_block + r
                w = jnp.exp2(m_scr[pg] - m_tot)
                if sel[j] is not None:
                    w = jnp.where(sel[j], w, 0.0)
                l_tot = l_tot + w * l_scr[pg]
                acc = acc + w * o_scr[pg]
        o_ref[0] = acc / l_tot


def _sample_attn(page_table, q, k_new, v_new, cache_k, cache_v, layer, bias, rel_bias_t):
    s, n_pages = page_table.shape
    pps = SAMPLE_PAGES_PER_STEP
    assert n_pages % pps == 0 and n_pages % (MOBA_BLOCK // PAGE_SIZE) == 0
    row = pl.BlockSpec((1, N_HEADS, HEAD_DIM), lambda si, st, pt: (si, 0, 0))

    def page(r):
        return pl.BlockSpec((None, None, PAGE_SIZE, N_HEADS, HEAD_DIM),
                            lambda si, st, pt: (layer, pt[si, st * pps + r], 0, 0, 0))

    stat = pltpu.VMEM((n_pages, N_HEADS, HEAD_DIM), F32)
    grid_spec = pltpu.PrefetchScalarGridSpec(
        num_scalar_prefetch=1, grid=(s, n_pages // pps),
        in_specs=[row, row, row] + [page(r) for r in range(pps)] * 2
        + [pl.BlockSpec((2, PAGE_SIZE, N_HEADS, HEAD_DIM), lambda si, st, pt: (0, 0, 0, 0)),
           pl.BlockSpec((N_HEADS, N_BUCKETS), lambda si, st, pt: (0, 0))],
        out_specs=row,
        scratch_shapes=[stat, stat, stat, stat],
    )
    return pl.pallas_call(
        _sample_attn_kernel, grid_spec=grid_spec,
        out_shape=jax.ShapeDtypeStruct((s, N_HEADS, HEAD_DIM), F32),
        compiler_params=_params("parallel", "arbitrary"),
        name="sample_attn",
    )(page_table, q, k_new, v_new, *([cache_k] * pps), *([cache_v] * pps), bias, rel_bias_t)


def _ln_swish(y, g, b):
    mu = jnp.mean(y, axis=-1, keepdims=True)
    yc = y - mu
    var = jnp.mean(yc * yc, axis=-1, keepdims=True)
    z = yc * lax.rsqrt(var + EPS) * g + b
    return z * jax.nn.sigmoid(z)


def _conv_prompt_kernel(cur_ref, halo_ref, w_ref, b_ref, g_ref, beta_ref, o_ref, xx_scr, pb_scr, y_scr):
    tt = cur_ref.shape[1]
    first = pl.program_id(1) == 0
    xx_scr[0:CONV_HALO, :] = jnp.where(first, 0.0, halo_ref[0])
    xx_scr[CONV_HALO:CONV_HALO + tt, :] = cur_ref[0]
    xx_scr[CONV_HALO + tt:, :] = jnp.zeros((SUBLANES, xx_scr.shape[1]), F32)
    lead = CONV_HALO - (CONV_K - 1)

    def body(cc, carry):
        cols = pl.ds(pl.multiple_of(cc * LANES, LANES), LANES)
        y = b_ref[:, cols]
        for b in range(SUBLANES):
            acc = None
            for a in range((lead + CONV_K - 1) // SUBLANES + 1):
                j = SUBLANES * a + b - lead
                if 0 <= j < CONV_K:
                    term = w_ref[j:j + 1, cols] * xx_scr[SUBLANES * a:SUBLANES * a + tt + SUBLANES, cols]
                    acc = term if acc is None else acc + term
            pb_scr[b] = acc
            y = y + pb_scr[b, b:b + tt, :]
        y_scr[:, cols] = y
        return carry

    lax.fori_loop(0, cur_ref.shape[2] // LANES, body, 0)
    o_ref[0] = _ln_swish(y_scr[...], g_ref[...], beta_ref[...]).astype(o_ref.dtype)


def _conv_prompt(u, w, b, g, beta, tt=256):
    bsz, t, c = u.shape
    vec = pl.BlockSpec((1, c), lambda bi, i: (0, 0))
    halo_blocks = tt // CONV_HALO
    return pl.pallas_call(
        _conv_prompt_kernel, grid=(bsz, t // tt),
        in_specs=[pl.BlockSpec((1, tt, c), lambda bi, i: (bi, i, 0)),
                  pl.BlockSpec((1, CONV_HALO, c), lambda bi, i: (bi, jnp.maximum(i * halo_blocks - 1, 0), 0)),
                  pl.BlockSpec((CONV_K, c), lambda bi, i: (0, 0)), vec, vec, vec],
        out_specs=pl.BlockSpec((1, tt, c), lambda bi, i: (bi, i, 0)),
        out_shape=jax.ShapeDtypeStruct((bsz, t, c), BF16),
        scratch_shapes=[pltpu.VMEM((CONV_HALO + tt + SUBLANES, c), F32),
                        pltpu.VMEM((SUBLANES, tt + SUBLANES, LANES), F32), pltpu.VMEM((tt, c), F32)],
        compiler_params=_params("parallel", "arbitrary"),
        name="conv_prompt",
    )(u, u, w, b, g, beta)


def _conv_sample_kernel(st_ref, u_ref, w_ref, b_ref, g_ref, beta_ref, y_ref, ns_ref):
    st = st_ref[...]
    u = u_ref[...]
    hist = CONV_K - 1
    y = jnp.sum(st * w_ref[0:hist, :][None], axis=1, keepdims=True) + u * w_ref[hist:CONV_K, :][None]
    y = y + b_ref[...][None]
    y_ref[...] = _ln_swish(y, g_ref[...][None], beta_ref[...][None])
    ns_ref[:, 0:hist - 1, :] = st_ref[:, 1:hist, :]
    ns_ref[:, hist - 1:hist, :] = u


def _conv_sample(state, u, w, b, g, beta, sb=16):
    s, hist, c = state.shape
    vec = pl.BlockSpec((1, c), lambda i: (0, 0))
    return pl.pallas_call(
        _conv_sample_kernel, grid=(s // sb,),
        in_specs=[pl.BlockSpec((sb, hist, c), lambda i: (i, 0, 0)),
                  pl.BlockSpec((sb, 1, c), lambda i: (i, 0, 0)),
                  pl.BlockSpec((CONV_K, c), lambda i: (0, 0)), vec, vec, vec],
        out_specs=[pl.BlockSpec((sb, 1, c), lambda i: (i, 0, 0)),
                   pl.BlockSpec((sb, hist, c), lambda i: (i, 0, 0))],
        out_shape=[jax.ShapeDtypeStruct((s, 1, c), F32), jax.ShapeDtypeStruct((s, hist, c), F32)],
        compiler_params=_params("parallel"),
        name="conv_sample",
    )(state, u, w, b, g, beta)


def _mix_kernel(x_ref, att_ref, yc_ref, h_ref, woa_ref, woc_ref, wga_ref, wgb_ref, wo_ref, o_ref):
    c = pl.program_id(1)

    def merged_chunk():
        h = h_ref[...]
        a_out = jnp.dot(att_ref[...], woa_ref[...], preferred_element_type=F32)
        c_out = jnp.dot(yc_ref[...], woc_ref[...], preferred_element_type=F32)
        ga = jnp.dot(h, wga_ref[...], preferred_element_type=F32)
        gb = jnp.dot(h, wgb_ref[...], preferred_element_type=F32)
        merged = (jax.nn.sigmoid(ga) * a_out + jax.nn.sigmoid(gb) * c_out).astype(BF16)
        return jnp.dot(merged, wo_ref[...], preferred_element_type=F32)

    @pl.when(c == 0)
    def _():
        o_ref[...] = x_ref[...] + merged_chunk()

    @pl.when(c > 0)
    def _():
        o_ref[...] += merged_chunk()


def _mix(x, att, yc, h, w_out_att, w_out_conv, w_in, ga_col0, gb_col0, w_out, tm):
    m = x.shape[0]
    tn = MIX_TN
    row = pl.BlockSpec((tm, D_MODEL), lambda i, c: (i, 0))

    def wcol(c0):
        return pl.BlockSpec((D_MODEL, tn), lambda i, c: (0, c0 // tn + c))

    return pl.pallas_call(
        _mix_kernel, grid=(m // tm, D_MODEL // tn),
        in_specs=[row, row, row, row, wcol(0), wcol(0), wcol(ga_col0), wcol(gb_col0),
                  pl.BlockSpec((tn, D_MODEL), lambda i, c: (c, 0))],
        out_specs=row,
        out_shape=jax.ShapeDtypeStruct((m, D_MODEL), F32),
        compiler_params=_params("parallel", "arbitrary"),
        name="mix",
    )(x, att, yc, h, w_out_att, w_out_conv, w_in, w_in, w_out)


def _layer(x, tm, w, attend_fn, conv_fn, q_dtype):
    x1, h = _ffn(x, w["ffn1_norm"], w["ffn1_w_gate"], w["ffn1_w_up"], w["ffn1_w_down"], w["mix_norm"], tm)
    q = _proj("headnorm", h, w["w_in"], 0, tm, q_dtype, gvec=w["q_norm"], scale=QK_SCALE)
    k = _proj("headnorm", h, w["w_in"], D_MODEL, tm, F32, gvec=w["k_norm"])
    v = _proj("plain", h, w["w_in"], 2 * D_MODEL, tm, F32)
    u = _proj("glu", h, w["w_in"], 3 * D_MODEL, tm, F32, col1=4 * D_MODEL)
    att = attend_fn(q, k, v)
    yc, conv_state = conv_fn(u)
    x2 = _mix(x1, att, yc, h, w["w_out_att"], w["w_out_conv"], w["w_in"], 5 * D_MODEL, 6 * D_MODEL,
              w["w_out"], tm)
    y = _ffn(x2, w["ffn2_norm"], w["ffn2_w_gate"], w["ffn2_w_up"], w["ffn2_w_down"], None, tm)
    return y, k, v, conv_state


def kernel(x_prompt, x_sample, cache_k, cache_v, state_conv, page_table, ffn1_norm, ffn1_w_gate, ffn1_w_up,
           ffn1_w_down, mix_norm, w_in, q_norm, k_norm, rel_bias, conv_dw_w, conv_dw_b, conv_ln_g, conv_ln_b,
           w_out_att, w_out_conv, w_out, ffn2_norm, ffn2_w_gate, ffn2_w_up, ffn2_w_down):
    depth = ffn1_norm.shape[0]
    assert depth == 1 and x_sample.shape[1] == 1
    bsz, seq, d = x_prompt.shape
    n_s = x_sample.shape[0]
    n_pages = page_table.shape[1]
    hist = CONV_K - 1
    l = 0
    w = {
        "ffn1_norm": ffn1_norm[l][None], "mix_norm": mix_norm[l][None], "ffn2_norm": ffn2_norm[l][None],
        "q_norm": q_norm[l][None], "k_norm": k_norm[l][None],
        "ffn1_w_gate": ffn1_w_gate[l].astype(BF16), "ffn1_w_up": ffn1_w_up[l].astype(BF16),
        "ffn1_w_down": ffn1_w_down[l].astype(BF16), "w_in": w_in[l].astype(BF16),
        "w_out_att": w_out_att[l].astype(BF16), "w_out_conv": w_out_conv[l].astype(BF16),
        "w_out": w_out[l].astype(BF16),
        "ffn2_w_gate": ffn2_w_gate[l].astype(BF16), "ffn2_w_up": ffn2_w_up[l].astype(BF16),
        "ffn2_w_down": ffn2_w_down[l].astype(BF16),
    }
    dw_w, dw_b = conv_dw_w[l], conv_dw_b[l][None]
    ln_g, ln_b = conv_ln_g[l][None], conv_ln_b[l][None]

    bias_p = _prompt_bias(rel_bias)

    def attend_prompt(q, k, v):
        shp = (bsz, seq, d)
        return _prompt_attn(q.reshape(shp), k.reshape(shp), v.reshape(shp), bias_p).reshape(bsz * seq, d)

    def conv_prompt(u):
        u3 = u.reshape(bsz, seq, d)
        yc = _conv_prompt(u3, dw_w, dw_b, ln_g, ln_b).reshape(bsz * seq, d)
        return yc, u3[:, seq - hist:, :]

    yp, kp, vp, cp = _layer(x_prompt.reshape(bsz * seq, d), 512, w, attend_prompt, conv_prompt, BF16)

    assert n_pages * PAGE_SIZE >= MAX_DISTANCE + PAGE_SIZE
    rel_bias_t = rel_bias.T
    bias_s = _sample_bias(rel_bias_t)

    def attend_sample(q, k, v):
        shp = (n_s, N_HEADS, HEAD_DIM)
        att = _sample_attn(page_table, q.reshape(shp), k.reshape(shp), v.reshape(shp), cache_k, cache_v, l,
                           bias_s, rel_bias_t)
        return att.reshape(n_s, d).astype(BF16)

    def conv_sample(u):
        yc, ns = _conv_sample(state_conv[l], u.reshape(n_s, 1, d), dw_w, dw_b, ln_g, ln_b)
        return yc.reshape(n_s, d).astype(BF16), ns

    ys, ks, vs, cs = _layer(x_sample.reshape(n_s, d), n_s, w, attend_sample, conv_sample, F32)

    heads = (N_HEADS, HEAD_DIM)
    return (yp.reshape(bsz, seq, d), ys.reshape(n_s, 1, d),
            kp.reshape(1, bsz, seq, *heads), vp.reshape(1, bsz, seq, *heads), cp[None],
            ks.reshape(1, n_s, 1, *heads), vs.reshape(1, n_s, 1, *heads), cs[None])
```

```python
import functools
import math

import numpy as np
import jax
import jax.numpy as jnp
from jax import lax
from jax.experimental import pallas as pl
from jax.experimental.pallas import tpu as pltpu

D_MODEL = 2048
N_HEADS = 16
HEAD_DIM = 128
D_FF = 5632
CONV_K = 31
MOBA_BLOCK = 256
MOBA_TOPK = 3
N_BUCKETS = 32
MAX_DISTANCE = 128
PAGE_SIZE = 128
EPS = 1e-6
NEG = -1e30
SCALE = HEAD_DIM ** -0.5
LOG2E = math.log2(math.e)
QK_SCALE = SCALE * LOG2E

F32 = jnp.float32
BF16 = jnp.bfloat16

VMEM_LIMIT_BYTES = 56 * 1024 * 1024
LANES = 128
SUBLANES = 8
BF16_ROWS = 16
CONV_HALO = 32
FFN_TF = 512
MIX_TN = 512
ATTN_LOOKAHEAD = 2
PROJ_COPY_TN = 512
MIX_COPY_TN = 256


def _params(*sem):
    return pltpu.CompilerParams(dimension_semantics=sem, vmem_limit_bytes=VMEM_LIMIT_BYTES)


def _t5_bucket_np(n, dtype):
    n = np.maximum(n, 0)
    max_exact = N_BUCKETS // 2
    ratio = np.log(np.maximum(n, 1).astype(dtype) / dtype(max_exact)) / dtype(math.log(MAX_DISTANCE / max_exact))
    large = max_exact + (ratio * dtype(N_BUCKETS - max_exact)).astype(np.int32)
    large = np.minimum(large, N_BUCKETS - 1)
    return np.where(n < max_exact, n, large)


def _bucket_thresholds():
    d = np.arange(0, 8192)
    b32 = _t5_bucket_np(d, np.float32)
    b64 = _t5_bucket_np(d, np.float64)
    assert np.array_equal(b32, b64) and np.all(np.diff(b32) >= 0)
    assert np.all(b32[MAX_DISTANCE:] == N_BUCKETS - 1)
    return [int(np.argmax(b32 >= t)) for t in range(N_BUCKETS)]


BUCKET_THR = _bucket_thresholds()


def _rms(x, g):
    ms = jnp.mean(x * x, axis=-1, keepdims=True)
    return x * lax.rsqrt(ms + EPS) * g


def _bf16_weight(w_ref, copy_ref):
    w = w_ref[...]
    if copy_ref is None:
        return w
    w = w.astype(BF16)
    copy_ref[...] = w
    return w


def _ffn_kernel(*refs, emit_norm, copy_weights):
    refs = list(refs)
    x_ref, g_ref, wg_ref, wu_ref, wd_ref = refs[:5]
    del refs[:5]
    gn_ref = refs.pop(0) if emit_norm else None
    o_ref = refs.pop(0)
    hn_ref = refs.pop(0) if emit_norm else None
    wg_copy, wu_copy, wd_copy = (refs.pop(0), refs.pop(0), refs.pop(0)) if copy_weights else (None, None, None)
    h_scr, = refs
    f = pl.program_id(1)
    last = pl.num_programs(1) - 1

    def hidden_chunk(h):
        gate = jnp.dot(h, _bf16_weight(wg_ref, wg_copy), preferred_element_type=F32)
        up = jnp.dot(h, _bf16_weight(wu_ref, wu_copy), preferred_element_type=F32)
        hid = (gate * jax.nn.sigmoid(gate) * up).astype(BF16)
        return jnp.dot(hid, _bf16_weight(wd_ref, wd_copy), preferred_element_type=F32)

    @pl.when(f == 0)
    def _():
        h = _rms(x_ref[...], g_ref[...]).astype(BF16)
        h_scr[...] = h
        o_ref[...] = hidden_chunk(h)

    @pl.when(jnp.logical_and(f > 0, f < last))
    def _():
        o_ref[...] += hidden_chunk(h_scr[...])

    @pl.when(f == last)
    def _():
        y = x_ref[...] + 0.5 * (o_ref[...] + hidden_chunk(h_scr[...]))
        o_ref[...] = y
        if emit_norm:
            hn_ref[...] = _rms(y, gn_ref[...]).astype(BF16)


def _ffn(x, g, wg, wu, wd, gn, tm, copy_weights):
    m = x.shape[0]
    tf = FFN_TF
    assert D_FF // tf >= 2
    assert not copy_weights or m == tm
    emit_norm = gn is not None
    row = pl.BlockSpec((tm, D_MODEL), lambda i, f: (i, 0))
    vec = pl.BlockSpec((1, D_MODEL), lambda i, f: (0, 0))
    wspecs = [pl.BlockSpec((D_MODEL, tf), lambda i, f: (0, f)),
              pl.BlockSpec((D_MODEL, tf), lambda i, f: (0, f)),
              pl.BlockSpec((tf, D_MODEL), lambda i, f: (f, 0))]
    in_specs = [row, vec] + wspecs
    args = [x, g, wg, wu, wd]
    out_shape = [jax.ShapeDtypeStruct((m, D_MODEL), F32)]
    out_specs = [row]
    if emit_norm:
        in_specs.append(vec)
        args.append(gn)
        out_shape.append(jax.ShapeDtypeStruct((m, D_MODEL), BF16))
        out_specs.append(row)
    if copy_weights:
        out_shape += [jax.ShapeDtypeStruct(w.shape, BF16) for w in (wg, wu, wd)]
        out_specs += wspecs
    out = list(pl.pallas_call(
        functools.partial(_ffn_kernel, emit_norm=emit_norm, copy_weights=copy_weights),
        grid=(m // tm, D_FF // tf),
        in_specs=in_specs, out_specs=out_specs, out_shape=out_shape,
        scratch_shapes=[pltpu.VMEM((tm, D_MODEL), BF16)],
        compiler_params=_params("parallel", "arbitrary"),
        name="ffn",
    )(*args))
    y = out.pop(0)
    hn = out.pop(0) if emit_norm else None
    return y, hn, (tuple(out) if copy_weights else None)


def _proj_headnorm_kernel(h_ref, w_ref, g_ref, o_ref, w_copy=None, *, scale):
    acc = jnp.dot(h_ref[...], _bf16_weight(w_ref, w_copy), preferred_element_type=F32)
    g = g_ref[...] * scale
    for s in range(acc.shape[1] // HEAD_DIM):
        sl = slice(s * HEAD_DIM, (s + 1) * HEAD_DIM)
        o_ref[:, sl] = _rms(acc[:, sl], g).astype(o_ref.dtype)


def _proj_plain_kernel(h_ref, w_ref, o_ref, w_copy=None):
    o_ref[...] = jnp.dot(h_ref[...], _bf16_weight(w_ref, w_copy), preferred_element_type=F32).astype(o_ref.dtype)


def _proj_glu_kernel(h_ref, wa_ref, wg_ref, o_ref, wa_copy=None, wg_copy=None):
    h = h_ref[...]
    a = jnp.dot(h, _bf16_weight(wa_ref, wa_copy), preferred_element_type=F32)
    g = jnp.dot(h, _bf16_weight(wg_ref, wg_copy), preferred_element_type=F32)
    o_ref[...] = a * jax.nn.sigmoid(g)


def _proj(kind, h, weights, tm, out_dtype, gvec=None, scale=1.0, copy_weights=False):
    m = h.shape[0]
    assert not copy_weights or m == tm
    if copy_weights:
        tn = PROJ_COPY_TN
    else:
        tn = D_MODEL // 2 if kind == "glu" else D_MODEL
    hspec = pl.BlockSpec((tm, D_MODEL), lambda i, j: (i, 0))
    ospec = pl.BlockSpec((tm, tn), lambda i, j: (i, j))

    def wspec(c0):
        return pl.BlockSpec((D_MODEL, tn), lambda i, j: (0, c0 // tn + j))

    in_specs = [hspec] + [wspec(c0) for _, c0 in weights]
    args = [h] + [w for w, _ in weights]
    if kind == "headnorm":
        body = functools.partial(_proj_headnorm_kernel, scale=scale)
        in_specs.append(pl.BlockSpec((1, HEAD_DIM), lambda i, j: (0, 0)))
        args.append(gvec)
    else:
        body = _proj_glu_kernel if kind == "glu" else _proj_plain_kernel
    out_shape = [jax.ShapeDtypeStruct((m, D_MODEL), out_dtype)]
    out_specs = [ospec]
    if copy_weights:
        out_shape += [jax.ShapeDtypeStruct((D_MODEL, D_MODEL), BF16)] * len(weights)
        out_specs += [wspec(0)] * len(weights)
    out = pl.pallas_call(
        body, grid=(m // tm, D_MODEL // tn), in_specs=in_specs, out_specs=out_specs, out_shape=out_shape,
        compiler_params=_params("parallel", "arbitrary"),
        name="proj_" + kind,
    )(*args)
    return out[0], (tuple(out[1:]) if copy_weights else None)


def _bias_chain(d, tab):
    val = tab(0)
    for t in range(1, N_BUCKETS):
        val = jnp.where(d >= BUCKET_THR[t], tab(t), val)
    return val


def _prompt_bias_kernel(rb_ref, o_ref):
    h = pl.program_id(0)
    shape = (MOBA_BLOCK, MOBA_BLOCK)
    d = lax.broadcasted_iota(jnp.int32, shape, 1) - lax.broadcasted_iota(jnp.int32, shape, 0)
    tab = lambda t: rb_ref[t, h] * LOG2E
    own = _bias_chain(jnp.maximum(d, 0), tab)
    o_ref[0, 0] = jnp.where(d >= 0, own, NEG)
    o_ref[0, 1] = _bias_chain(d + MOBA_BLOCK, tab)
    o_ref[0, 2] = jnp.full(shape, tab(N_BUCKETS - 1), F32)


def _prompt_bias(rel_bias):
    return pl.pallas_call(
        _prompt_bias_kernel, grid=(N_HEADS,),
        in_specs=[pl.BlockSpec(memory_space=pltpu.SMEM)],
        out_specs=pl.BlockSpec((1, 3, MOBA_BLOCK, MOBA_BLOCK), lambda h: (h, 0, 0, 0)),
        out_shape=jax.ShapeDtypeStruct((N_HEADS, 3, MOBA_BLOCK, MOBA_BLOCK), F32),
        compiler_params=_params("arbitrary"),
        name="prompt_bias",
    )(rel_bias)


def _sample_bias_kernel(rbt_ref, o_ref):
    shape = (PAGE_SIZE, N_HEADS, HEAD_DIM)
    tab = lambda t: rbt_ref[:, t:t + 1][None] * LOG2E
    d = PAGE_SIZE - lax.broadcasted_iota(jnp.int32, shape, 0)
    o_ref[0] = jnp.broadcast_to(tab(N_BUCKETS - 1), shape)
    o_ref[1] = jnp.broadcast_to(_bias_chain(d, tab), shape)


def _sample_bias(rel_bias_t):
    return pl.pallas_call(
        _sample_bias_kernel,
        out_shape=jax.ShapeDtypeStruct((2, PAGE_SIZE, N_HEADS, HEAD_DIM), F32),
        name="sample_bias",
    )(rel_bias_t)


def _topk_select(gates, k_sel):
    n = len(gates)
    if n <= k_sel:
        return [None] * n
    sel = []
    for j in range(n):
        cnt = jnp.zeros(gates[j].shape, F32)
        for j2 in range(n):
            if j2 == j:
                continue
            ahead = (gates[j2] >= gates[j]) if j2 < j else (gates[j2] > gates[j])
            cnt = cnt + jnp.where(ahead, 1.0, 0.0)
        sel.append(cnt < k_sel)
    return sel


def _prompt_attn_kernel(q_ref, k_ref, v_ref, bias_ref, o_ref, kb_scr, vt_scr, s_scr, p_scr):
    t = q_ref.shape[1]
    nb = t // MOBA_BLOCK
    nbuf = s_scr.shape[0]
    nt = (((1,), (1,)), ((), ()))
    blk = lambda j: slice(j * MOBA_BLOCK, (j + 1) * MOBA_BLOCK)
    k = k_ref[0]
    kb_scr[...] = k.astype(BF16)
    vt_scr[0:HEAD_DIM, :] = v_ref[0].T.astype(BF16)
    vt_scr[HEAD_DIM:, :] = jnp.ones((BF16_ROWS, t), BF16)
    ksum = jnp.sum(k.reshape(nb, MOBA_BLOCK, HEAD_DIM), axis=1)
    ks_hi = ksum.astype(BF16)
    ks_lo = (ksum - ks_hi.astype(F32)).astype(BF16)
    far_row = bias_ref[0, 2][0:1, :]
    qs = [q_ref[0, blk(i), :] for i in range(nb)]
    masks = {}
    for i in range(MOBA_TOPK + 1, nb):
        g = (lax.dot_general(ks_hi, qs[i], nt, preferred_element_type=F32)
             + lax.dot_general(ks_lo, qs[i], nt, preferred_element_type=F32))
        sel = _topk_select([g[j:j + 1, :] for j in range(i)], MOBA_TOPK)
        masks[i] = [jnp.where(sj, 0.0, NEG) for sj in sel]

    def biased_scores(i):
        nk = (i + 1) * MOBA_BLOCK
        sv = lax.dot_general(kb_scr[0:nk, :], qs[i], nt, preferred_element_type=F32)
        m = None
        for j in range(i + 1):
            tile = sv[blk(j), :]
            mask = masks[i][j] if (i in masks and j < i) else None
            if j == i:
                tile = tile + bias_ref[0, 0]
            elif j == i - 1:
                tile = tile + bias_ref[0, 1]
                if mask is not None:
                    tile = tile + mask
            else:
                tile = tile + (far_row if mask is None else far_row + mask)
            s_scr[i % nbuf, blk(j), :] = tile
            cm = jnp.max(tile, axis=0, keepdims=True)
            m = cm if m is None else jnp.maximum(m, cm)
        return m

    col_max = {i: biased_scores(i) for i in range(min(ATTN_LOOKAHEAD, nb))}
    for i in range(nb):
        if i + ATTN_LOOKAHEAD < nb:
            col_max[i + ATTN_LOOKAHEAD] = biased_scores(i + ATTN_LOOKAHEAD)
        m = col_max.pop(i)
        nk = (i + 1) * MOBA_BLOCK
        for j in range(i + 1):
            p_scr[i % nbuf, blk(j), :] = jnp.exp2(s_scr[i % nbuf, blk(j), :] - m).astype(BF16)
        acc = jnp.dot(vt_scr[:, 0:nk], p_scr[i % nbuf, 0:nk, :], preferred_element_type=F32)
        out = acc[0:HEAD_DIM, :] / acc[HEAD_DIM:HEAD_DIM + 1, :]
        o_ref[0, blk(i), :] = out.T.astype(o_ref.dtype)


def _prompt_attn(q, k, v, bias):
    b, t, _ = q.shape
    spec = pl.BlockSpec((1, t, HEAD_DIM), lambda bi, h: (bi, 0, h))
    return pl.pallas_call(
        _prompt_attn_kernel, grid=(b, N_HEADS),
        in_specs=[spec, spec, spec,
                  pl.BlockSpec((1, 3, MOBA_BLOCK, MOBA_BLOCK), lambda bi, h: (h, 0, 0, 0))],
        out_specs=spec,
        out_shape=jax.ShapeDtypeStruct((b, t, N_HEADS * HEAD_DIM), BF16),
        scratch_shapes=[pltpu.VMEM((t, HEAD_DIM), BF16), pltpu.VMEM((HEAD_DIM + BF16_ROWS, t), BF16),
                        pltpu.VMEM((ATTN_LOOKAHEAD + 1, t, MOBA_BLOCK), F32),
                        pltpu.VMEM((ATTN_LOOKAHEAD + 1, t, MOBA_BLOCK), BF16)],
        compiler_params=_params("parallel", "arbitrary"),
        name="prompt_attn",
    )(q, k, v, bias)


SAMPLE_PAGES_PER_STEP = 8


def _sample_attn_kernel(pt_ref, q_ref, kn_ref, vn_ref, *refs):
    del pt_ref
    pps = SAMPLE_PAGES_PER_STEP
    k_refs, v_refs = refs[:pps], refs[pps:2 * pps]
    bias_ref, rbt_ref, o_ref, m_scr, l_scr, g_scr, o_scr = refs[2 * pps:]
    step = pl.program_id(1)
    n_pages = o_scr.shape[0]
    hd = (N_HEADS, HEAD_DIM)
    q = q_ref[0]

    last_step = step == pl.num_programs(1) - 1
    far = rbt_ref[:, N_BUCKETS - 1:N_BUCKETS] * LOG2E
    for r in range(pps):
        page = step * pps + r
        s = jnp.sum(k_refs[r][...] * q[None], axis=-1, keepdims=True)
        if r < pps - 1:
            logits = s
            m = jnp.max(s, axis=0)
            m_scr[page] = jnp.broadcast_to(m + far, hd)
        else:
            logits = s + bias_ref[jnp.where(last_step, 1, 0)]
            m = jnp.max(logits, axis=0)
            m_scr[page] = m
        e = jnp.exp2(logits - m[None])
        g_scr[page] = jnp.broadcast_to(jnp.sum(s, axis=0), hd)
        l_scr[page] = jnp.broadcast_to(jnp.sum(e, axis=0), hd)
        o_scr[page] = jnp.sum(e * v_refs[r][...], axis=0)

    @pl.when(last_step)
    def _():
        pages_per_block = MOBA_BLOCK // PAGE_SIZE
        n_blocks = n_pages // pages_per_block
        gates = [sum(g_scr[j * pages_per_block + r] for r in range(pages_per_block)) for j in range(n_blocks)]
        sel = _topk_select(gates, MOBA_TOPK)
        s_own = jnp.sum(q * kn_ref[0], axis=-1, keepdims=True) + rbt_ref[:, 0:1] * LOG2E
        m_tot = jnp.broadcast_to(s_own, hd)
        for j in range(n_blocks):
            for r in range(pages_per_block):
                mp = m_scr[j * pages_per_block + r]
                if sel[j] is not None:
                    mp = jnp.where(sel[j], mp, NEG)
                m_tot = jnp.maximum(m_tot, mp)
        w_own = jnp.exp2(s_own - m_tot)
        l_tot = w_own
        acc = w_own * vn_ref[0]
        for j in range(n_blocks):
            for r in range(pages_per_block):
                pg = j * pages_per_block + r
                w = jnp.exp2(m_scr[pg] - m_tot)
                if sel[j] is not None:
                    w = jnp.where(sel[j], w, 0.0)
                l_tot = l_tot + w * l_scr[pg]
                acc = acc + w * o_scr[pg]
        o_ref[0] = acc / l_tot


def _sample_attn(page_table, q, k_new, v_new, cache_k, cache_v, layer, bias, rel_bias_t):
    s, n_pages = page_table.shape
    pps = SAMPLE_PAGES_PER_STEP
    assert n_pages % pps == 0 and n_pages % (MOBA_BLOCK // PAGE_SIZE) == 0
    row = pl.BlockSpec((1, N_HEADS, HEAD_DIM), lambda si, st, pt: (si, 0, 0))

    def page(r):
        return pl.BlockSpec((None, None, PAGE_SIZE, N_HEADS, HEAD_DIM),
                            lambda si, st, pt: (layer, pt[si, st * pps + r], 0, 0, 0))

    stat = pltpu.VMEM((n_pages, N_HEADS, HEAD_DIM), F32)
    grid_spec = pltpu.PrefetchScalarGridSpec(
        num_scalar_prefetch=1, grid=(s, n_pages // pps),
        in_specs=[row, row, row] + [page(r) for r in range(pps)] * 2
        + [pl.BlockSpec((2, PAGE_SIZE, N_HEADS, HEAD_DIM), lambda si, st, pt: (0, 0, 0, 0)),
           pl.BlockSpec((N_HEADS, N_BUCKETS), lambda si, st, pt: (0, 0))],
        out_specs=row,
        scratch_shapes=[stat, stat, stat, stat],
    )
    return pl.pallas_call(
        _sample_attn_kernel, grid_spec=grid_spec,
        out_shape=jax.ShapeDtypeStruct((s, N_HEADS, HEAD_DIM), F32),
        compiler_params=_params("parallel", "arbitrary"),
        name="sample_attn",
    )(page_table, q, k_new, v_new, *([cache_k] * pps), *([cache_v] * pps), bias, rel_bias_t)


def _ln_swish(y, g, b):
    mu = jnp.mean(y, axis=-1, keepdims=True)
    yc = y - mu
    var = jnp.mean(yc * yc, axis=-1, keepdims=True)
    z = yc * lax.rsqrt(var + EPS) * g + b
    return z * jax.nn.sigmoid(z)


def _conv_prompt_kernel(cur_ref, halo_ref, w_ref, b_ref, g_ref, beta_ref, o_ref, xx_scr, pb_scr, y_scr):
    tt = cur_ref.shape[1]
    first = pl.program_id(1) == 0
    xx_scr[0:CONV_HALO, :] = jnp.where(first, 0.0, halo_ref[0])
    xx_scr[CONV_HALO:CONV_HALO + tt, :] = cur_ref[0]
    xx_scr[CONV_HALO + tt:, :] = jnp.zeros((SUBLANES, xx_scr.shape[1]), F32)
    lead = CONV_HALO - (CONV_K - 1)

    def body(cc, carry):
        cols = pl.ds(pl.multiple_of(cc * LANES, LANES), LANES)
        y = b_ref[:, cols]
        for b in range(SUBLANES):
            acc = None
            for a in range((lead + CONV_K - 1) // SUBLANES + 1):
                j = SUBLANES * a + b - lead
                if 0 <= j < CONV_K:
                    term = w_ref[j:j + 1, cols] * xx_scr[SUBLANES * a:SUBLANES * a + tt + SUBLANES, cols]
                    acc = term if acc is None else acc + term
            pb_scr[b] = acc
            y = y + pb_scr[b, b:b + tt, :]
        y_scr[:, cols] = y
        return carry

    lax.fori_loop(0, cur_ref.shape[2] // LANES, body, 0)
    o_ref[0] = _ln_swish(y_scr[...], g_ref[...], beta_ref[...]).astype(o_ref.dtype)


def _conv_prompt(u, w, b, g, beta, tt=256):
    bsz, t, c = u.shape
    vec = pl.BlockSpec((1, c), lambda bi, i: (0, 0))
    halo_blocks = tt // CONV_HALO
    return pl.pallas_call(
        _conv_prompt_kernel, grid=(bsz, t // tt),
        in_specs=[pl.BlockSpec((1, tt, c), lambda bi, i: (bi, i, 0)),
                  pl.BlockSpec((1, CONV_HALO, c), lambda bi, i: (bi, jnp.maximum(i * halo_blocks - 1, 0), 0)),
                  pl.BlockSpec((CONV_K, c), lambda bi, i: (0, 0)), vec, vec, vec],
        out_specs=pl.BlockSpec((1, tt, c), lambda bi, i: (bi, i, 0)),
        out_shape=jax.ShapeDtypeStruct((bsz, t, c), BF16),
        scratch_shapes=[pltpu.VMEM((CONV_HALO + tt + SUBLANES, c), F32),
                        pltpu.VMEM((SUBLANES, tt + SUBLANES, LANES), F32), pltpu.VMEM((tt, c), F32)],
        compiler_params=_params("parallel", "arbitrary"),
        name="conv_prompt",
    )(u, u, w, b, g, beta)


def _conv_sample_kernel(st_ref, u_ref, w_ref, b_ref, g_ref, beta_ref, y_ref, ns_ref):
    st = st_ref[...]
    u = u_ref[...]
    hist = CONV_K - 1
    y = jnp.sum(st * w_ref[0:hist, :][None], axis=1, keepdims=True) + u * w_ref[hist:CONV_K, :][None]
    y = y + b_ref[...][None]
    y_ref[...] = _ln_swish(y, g_ref[...][None], beta_ref[...][None])
    ns_ref[:, 0:hist - 1, :] = st_ref[:, 1:hist, :]
    ns_ref[:, hist - 1:hist, :] = u


def _conv_sample(state, u, w, b, g, beta, sb=16):
    s, hist, c = state.shape
    vec = pl.BlockSpec((1, c), lambda i: (0, 0))
    return pl.pallas_call(
        _conv_sample_kernel, grid=(s // sb,),
        in_specs=[pl.BlockSpec((sb, hist, c), lambda i: (i, 0, 0)),
                  pl.BlockSpec((sb, 1, c), lambda i: (i, 0, 0)),
                  pl.BlockSpec((CONV_K, c), lambda i: (0, 0)), vec, vec, vec],
        out_specs=[pl.BlockSpec((sb, 1, c), lambda i: (i, 0, 0)),
                   pl.BlockSpec((sb, hist, c), lambda i: (i, 0, 0))],
        out_shape=[jax.ShapeDtypeStruct((s, 1, c), F32), jax.ShapeDtypeStruct((s, hist, c), F32)],
        compiler_params=_params("parallel"),
        name="conv_sample",
    )(state, u, w, b, g, beta)


def _mix_kernel(x_ref, att_ref, yc_ref, h_ref, woa_ref, woc_ref, wga_ref, wgb_ref, wo_ref, o_ref, *copies):
    c = pl.program_id(1)
    woa_copy, woc_copy, wga_copy, wgb_copy, wo_copy = copies if copies else (None,) * 5

    def merged_chunk():
        h = h_ref[...]
        a_out = jnp.dot(att_ref[...], _bf16_weight(woa_ref, woa_copy), preferred_element_type=F32)
        c_out = jnp.dot(yc_ref[...], _bf16_weight(woc_ref, woc_copy), preferred_element_type=F32)
        ga = jnp.dot(h, _bf16_weight(wga_ref, wga_copy), preferred_element_type=F32)
        gb = jnp.dot(h, _bf16_weight(wgb_ref, wgb_copy), preferred_element_type=F32)
        merged = (jax.nn.sigmoid(ga) * a_out + jax.nn.sigmoid(gb) * c_out).astype(BF16)
        return jnp.dot(merged, _bf16_weight(wo_ref, wo_copy), preferred_element_type=F32)

    @pl.when(c == 0)
    def _():
        o_ref[...] = x_ref[...] + merged_chunk()

    @pl.when(c > 0)
    def _():
        o_ref[...] += merged_chunk()


def _mix(x, att, yc, h, w_out_att, w_out_conv, w_ga, w_gb, w_out, tm, copy_weights):
    m = x.shape[0]
    assert not copy_weights or m == tm
    tn = MIX_COPY_TN if copy_weights else MIX_TN
    row = pl.BlockSpec((tm, D_MODEL), lambda i, c: (i, 0))

    def wcol(c0):
        return pl.BlockSpec((D_MODEL, tn), lambda i, c: (0, c0 // tn + c))

    wrow = pl.BlockSpec((tn, D_MODEL), lambda i, c: (c, 0))
    out_shape = [jax.ShapeDtypeStruct((m, D_MODEL), F32)]
    out_specs = [row]
    if copy_weights:
        out_shape += [jax.ShapeDtypeStruct((D_MODEL, D_MODEL), BF16)] * 5
        out_specs += [wcol(0)] * 4 + [wrow]
    out = pl.pallas_call(
        _mix_kernel, grid=(m // tm, D_MODEL // tn),
        in_specs=[row, row, row, row, wcol(0), wcol(0), wcol(w_ga[1]), wcol(w_gb[1]), wrow],
        out_specs=out_specs, out_shape=out_shape,
        compiler_params=_params("parallel", "arbitrary"),
        name="mix",
    )(x, att, yc, h, w_out_att, w_out_conv, w_ga[0], w_gb[0], w_out)
    return out[0], (tuple(out[1:]) if copy_weights else None)


def _layer(x, tm, norms, w, attend_fn, conv_fn, q_dtype, copy_weights):
    cw = copy_weights
    x1, h, c_ffn1 = _ffn(x, norms["ffn1"], *w["ffn1"], norms["mix"], tm, cw)
    q, c_q = _proj("headnorm", h, [w["q"]], tm, q_dtype, gvec=norms["q"], scale=QK_SCALE, copy_weights=cw)
    k, c_k = _proj("headnorm", h, [w["k"]], tm, F32, gvec=norms["k"], copy_weights=cw)
    v, c_v = _proj("plain", h, [w["v"]], tm, F32, copy_weights=cw)
    u, c_u = _proj("glu", h, [w["a"], w["g"]], tm, F32, copy_weights=cw)
    att = attend_fn(q, k, v)
    yc, conv_state = conv_fn(u)
    x2, c_mix = _mix(x1, att, yc, h, w["out_att"], w["out_conv"], w["ga"], w["gb"], w["out"], tm, cw)
    y, _, c_ffn2 = _ffn(x2, norms["ffn2"], *w["ffn2"], None, tm, cw)
    copies = None
    if cw:
        copies = {"ffn1": c_ffn1, "ffn2": c_ffn2, "q": (c_q[0], 0), "k": (c_k[0], 0), "v": (c_v[0], 0),
                  "a": (c_u[0], 0), "g": (c_u[1], 0), "out_att": c_mix[0], "out_conv": c_mix[1],
                  "ga": (c_mix[2], 0), "gb": (c_mix[3], 0), "out": c_mix[4]}
    return y, k, v, conv_state, copies


def kernel(x_prompt, x_sample, cache_k, cache_v, state_conv, page_table, ffn1_norm, ffn1_w_gate, ffn1_w_up,
           ffn1_w_down, mix_norm, w_in, q_norm, k_norm, rel_bias, conv_dw_w, conv_dw_b, conv_ln_g, conv_ln_b,
           w_out_att, w_out_conv, w_out, ffn2_norm, ffn2_w_gate, ffn2_w_up, ffn2_w_down):
    depth = ffn1_norm.shape[0]
    assert depth == 1 and x_sample.shape[1] == 1
    bsz, seq, d = x_prompt.shape
    n_s = x_sample.shape[0]
    n_pages = page_table.shape[1]
    hist = CONV_K - 1
    l = 0
    norms = {"ffn1": ffn1_norm[l][None], "mix": mix_norm[l][None], "ffn2": ffn2_norm[l][None],
             "q": q_norm[l][None], "k": k_norm[l][None]}
    w_f32 = {"ffn1": (ffn1_w_gate[l], ffn1_w_up[l], ffn1_w_down[l]),
             "ffn2": (ffn2_w_gate[l], ffn2_w_up[l], ffn2_w_down[l]),
             "out_att": w_out_att[l], "out_conv": w_out_conv[l], "out": w_out[l]}
    for n, name in enumerate(("q", "k", "v", "a", "g", "ga", "gb")):
        w_f32[name] = (w_in[l], n * D_MODEL)
    dw_w, dw_b = conv_dw_w[l], conv_dw_b[l][None]
    ln_g, ln_b = conv_ln_g[l][None], conv_ln_b[l][None]

    assert n_pages * PAGE_SIZE >= MAX_DISTANCE + PAGE_SIZE
    rel_bias_t = rel_bias.T
    bias_s = _sample_bias(rel_bias_t)

    def attend_sample(q, k, v):
        shp = (n_s, N_HEADS, HEAD_DIM)
        att = _sample_attn(page_table, q.reshape(shp), k.reshape(shp), v.reshape(shp), cache_k, cache_v, l,
                           bias_s, rel_bias_t)
        return att.reshape(n_s, d).astype(BF16)

    def conv_sample(u):
        yc, ns = _conv_sample(state_conv[l], u.reshape(n_s, 1, d), dw_w, dw_b, ln_g, ln_b)
        return yc.reshape(n_s, d).astype(BF16), ns

    ys, ks, vs, cs, w_bf16 = _layer(x_sample.reshape(n_s, d), n_s, norms, w_f32, attend_sample, conv_sample, F32,
                                    copy_weights=True)

    bias_p = _prompt_bias(rel_bias)

    def attend_prompt(q, k, v):
        shp = (bsz, seq, d)
        return _prompt_attn(q.reshape(shp), k.reshape(shp), v.reshape(shp), bias_p).reshape(bsz * seq, d)

    def conv_prompt(u):
        u3 = u.reshape(bsz, seq, d)
        yc = _conv_prompt(u3, dw_w, dw_b, ln_g, ln_b).reshape(bsz * seq, d)
        return yc, u3[:, seq - hist:, :]

    yp, kp, vp, cp, _ = _layer(x_prompt.reshape(bsz * seq, d), 512, norms, w_bf16, attend_prompt, conv_prompt, BF16,
                               copy_weights=False)

    heads = (N_HEADS, HEAD_DIM)
    return (yp.reshape(bsz, seq, d), ys.reshape(n_s, 1, d),
            kp.reshape(1, bsz, seq, *heads), vp.reshape(1, bsz, seq, *heads), cp[None],
            ks.reshape(1, n_s, 1, *heads), vs.reshape(1, n_s, 1, *heads), cs[None])
```

```python
import functools
import math

import numpy as np
import jax
import jax.numpy as jnp
from jax import lax
from jax.experimental import pallas as pl
from jax.experimental.pallas import tpu as pltpu

D_MODEL = 2048
N_HEADS = 16
HEAD_DIM = 128
D_FF = 5632
CONV_K = 31
MOBA_BLOCK = 256
MOBA_TOPK = 3
N_BUCKETS = 32
MAX_DISTANCE = 128
PAGE_SIZE = 128
EPS = 1e-6
NEG = -1e30
SCALE = HEAD_DIM ** -0.5
LOG2E = math.log2(math.e)
QK_SCALE = SCALE * LOG2E

F32 = jnp.float32
BF16 = jnp.bfloat16

VMEM_LIMIT_BYTES = 56 * 1024 * 1024
LANES = 128
SUBLANES = 8
BF16_ROWS = 16
CONV_HALO = 32
FFN_TF = 512
MIX_TN = 512
ATTN_LOOKAHEAD = 1
ATTN_HEADS = 2
PROJ_COPY_TN = 512
MIX_COPY_TN = 256


def _params(*sem):
    return pltpu.CompilerParams(dimension_semantics=sem, vmem_limit_bytes=VMEM_LIMIT_BYTES)


def _t5_bucket_np(n, dtype):
    n = np.maximum(n, 0)
    max_exact = N_BUCKETS // 2
    ratio = np.log(np.maximum(n, 1).astype(dtype) / dtype(max_exact)) / dtype(math.log(MAX_DISTANCE / max_exact))
    large = max_exact + (ratio * dtype(N_BUCKETS - max_exact)).astype(np.int32)
    large = np.minimum(large, N_BUCKETS - 1)
    return np.where(n < max_exact, n, large)


def _bucket_thresholds():
    d = np.arange(0, 8192)
    b32 = _t5_bucket_np(d, np.float32)
    b64 = _t5_bucket_np(d, np.float64)
    assert np.array_equal(b32, b64) and np.all(np.diff(b32) >= 0)
    assert np.all(b32[MAX_DISTANCE:] == N_BUCKETS - 1)
    return [int(np.argmax(b32 >= t)) for t in range(N_BUCKETS)]


BUCKET_THR = _bucket_thresholds()


def _rms(x, g):
    ms = jnp.mean(x * x, axis=-1, keepdims=True)
    return x * lax.rsqrt(ms + EPS) * g


def _bf16_weight(w_ref, copy_ref):
    w = w_ref[...]
    if copy_ref is None:
        return w
    w = w.astype(BF16)
    copy_ref[...] = w
    return w


def _ffn_kernel(*refs, emit_norm, copy_weights):
    refs = list(refs)
    x_ref, g_ref, wg_ref, wu_ref, wd_ref = refs[:5]
    del refs[:5]
    gn_ref = refs.pop(0) if emit_norm else None
    o_ref = refs.pop(0)
    hn_ref = refs.pop(0) if emit_norm else None
    wg_copy, wu_copy, wd_copy = (refs.pop(0), refs.pop(0), refs.pop(0)) if copy_weights else (None, None, None)
    h_scr, = refs
    f = pl.program_id(1)
    last = pl.num_programs(1) - 1

    def hidden_chunk(h):
        gate = jnp.dot(h, _bf16_weight(wg_ref, wg_copy), preferred_element_type=F32)
        up = jnp.dot(h, _bf16_weight(wu_ref, wu_copy), preferred_element_type=F32)
        hid = (gate * jax.nn.sigmoid(gate) * up).astype(BF16)
        return jnp.dot(hid, _bf16_weight(wd_ref, wd_copy), preferred_element_type=F32)

    @pl.when(f == 0)
    def _():
        h = _rms(x_ref[...], g_ref[...]).astype(BF16)
        h_scr[...] = h
        o_ref[...] = hidden_chunk(h)

    @pl.when(jnp.logical_and(f > 0, f < last))
    def _():
        o_ref[...] += hidden_chunk(h_scr[...])

    @pl.when(f == last)
    def _():
        y = x_ref[...] + 0.5 * (o_ref[...] + hidden_chunk(h_scr[...]))
        o_ref[...] = y
        if emit_norm:
            hn_ref[...] = _rms(y, gn_ref[...]).astype(BF16)


def _ffn(x, g, wg, wu, wd, gn, tm, copy_weights):
    m = x.shape[0]
    tf = FFN_TF
    assert D_FF // tf >= 2
    assert not copy_weights or m == tm
    emit_norm = gn is not None
    row = pl.BlockSpec((tm, D_MODEL), lambda i, f: (i, 0))
    vec = pl.BlockSpec((1, D_MODEL), lambda i, f: (0, 0))
    wspecs = [pl.BlockSpec((D_MODEL, tf), lambda i, f: (0, f)),
              pl.BlockSpec((D_MODEL, tf), lambda i, f: (0, f)),
              pl.BlockSpec((tf, D_MODEL), lambda i, f: (f, 0))]
    in_specs = [row, vec] + wspecs
    args = [x, g, wg, wu, wd]
    out_shape = [jax.ShapeDtypeStruct((m, D_MODEL), F32)]
    out_specs = [row]
    if emit_norm:
        in_specs.append(vec)
        args.append(gn)
        out_shape.append(jax.ShapeDtypeStruct((m, D_MODEL), BF16))
        out_specs.append(row)
    if copy_weights:
        out_shape += [jax.ShapeDtypeStruct(w.shape, BF16) for w in (wg, wu, wd)]
        out_specs += wspecs
    out = list(pl.pallas_call(
        functools.partial(_ffn_kernel, emit_norm=emit_norm, copy_weights=copy_weights),
        grid=(m // tm, D_FF // tf),
        in_specs=in_specs, out_specs=out_specs, out_shape=out_shape,
        scratch_shapes=[pltpu.VMEM((tm, D_MODEL), BF16)],
        compiler_params=_params("parallel", "arbitrary"),
        name="ffn",
    )(*args))
    y = out.pop(0)
    hn = out.pop(0) if emit_norm else None
    return y, hn, (tuple(out) if copy_weights else None)


def _proj_headnorm_kernel(h_ref, w_ref, g_ref, o_ref, w_copy=None, *, scale):
    acc = jnp.dot(h_ref[...], _bf16_weight(w_ref, w_copy), preferred_element_type=F32)
    g = g_ref[...] * scale
    for s in range(acc.shape[1] // HEAD_DIM):
        sl = slice(s * HEAD_DIM, (s + 1) * HEAD_DIM)
        o_ref[:, sl] = _rms(acc[:, sl], g).astype(o_ref.dtype)


def _proj_plain_kernel(h_ref, w_ref, o_ref, w_copy=None):
    o_ref[...] = jnp.dot(h_ref[...], _bf16_weight(w_ref, w_copy), preferred_element_type=F32).astype(o_ref.dtype)


def _proj_glu_kernel(h_ref, wa_ref, wg_ref, o_ref, wa_copy=None, wg_copy=None):
    h = h_ref[...]
    a = jnp.dot(h, _bf16_weight(wa_ref, wa_copy), preferred_element_type=F32)
    g = jnp.dot(h, _bf16_weight(wg_ref, wg_copy), preferred_element_type=F32)
    o_ref[...] = a * jax.nn.sigmoid(g)


def _proj(kind, h, weights, tm, out_dtype, gvec=None, scale=1.0, copy_weights=False):
    m = h.shape[0]
    assert not copy_weights or m == tm
    if copy_weights:
        tn = PROJ_COPY_TN
    else:
        tn = D_MODEL // 2 if kind == "glu" else D_MODEL
    hspec = pl.BlockSpec((tm, D_MODEL), lambda i, j: (i, 0))
    ospec = pl.BlockSpec((tm, tn), lambda i, j: (i, j))

    def wspec(c0):
        return pl.BlockSpec((D_MODEL, tn), lambda i, j: (0, c0 // tn + j))

    in_specs = [hspec] + [wspec(c0) for _, c0 in weights]
    args = [h] + [w for w, _ in weights]
    if kind == "headnorm":
        body = functools.partial(_proj_headnorm_kernel, scale=scale)
        in_specs.append(pl.BlockSpec((1, HEAD_DIM), lambda i, j: (0, 0)))
        args.append(gvec)
    else:
        body = _proj_glu_kernel if kind == "glu" else _proj_plain_kernel
    out_shape = [jax.ShapeDtypeStruct((m, D_MODEL), out_dtype)]
    out_specs = [ospec]
    if copy_weights:
        out_shape += [jax.ShapeDtypeStruct((D_MODEL, D_MODEL), BF16)] * len(weights)
        out_specs += [wspec(0)] * len(weights)
    out = pl.pallas_call(
        body, grid=(m // tm, D_MODEL // tn), in_specs=in_specs, out_specs=out_specs, out_shape=out_shape,
        compiler_params=_params("parallel", "arbitrary"),
        name="proj_" + kind,
    )(*args)
    return out[0], (tuple(out[1:]) if copy_weights else None)


def _bias_chain(d, tab):
    val = tab(0)
    for t in range(1, N_BUCKETS):
        val = jnp.where(d >= BUCKET_THR[t], tab(t), val)
    return val


def _prompt_bias_kernel(rb_ref, o_ref):
    h = pl.program_id(0)
    shape = (MOBA_BLOCK, MOBA_BLOCK)
    d = lax.broadcasted_iota(jnp.int32, shape, 1) - lax.broadcasted_iota(jnp.int32, shape, 0)
    tab = lambda t: rb_ref[t, h] * LOG2E
    own = _bias_chain(jnp.maximum(d, 0), tab)
    o_ref[0, 0] = jnp.where(d >= 0, own, NEG)
    o_ref[0, 1] = _bias_chain(d + MOBA_BLOCK, tab)
    o_ref[0, 2] = jnp.full(shape, tab(N_BUCKETS - 1), F32)


def _prompt_bias(rel_bias):
    return pl.pallas_call(
        _prompt_bias_kernel, grid=(N_HEADS,),
        in_specs=[pl.BlockSpec(memory_space=pltpu.SMEM)],
        out_specs=pl.BlockSpec((1, 3, MOBA_BLOCK, MOBA_BLOCK), lambda h: (h, 0, 0, 0)),
        out_shape=jax.ShapeDtypeStruct((N_HEADS, 3, MOBA_BLOCK, MOBA_BLOCK), F32),
        compiler_params=_params("arbitrary"),
        name="prompt_bias",
    )(rel_bias)


def _sample_bias_kernel(rbt_ref, o_ref):
    shape = (PAGE_SIZE, N_HEADS, HEAD_DIM)
    tab = lambda t: rbt_ref[:, t:t + 1][None] * LOG2E
    d = PAGE_SIZE - lax.broadcasted_iota(jnp.int32, shape, 0)
    o_ref[0] = jnp.broadcast_to(tab(N_BUCKETS - 1), shape)
    o_ref[1] = jnp.broadcast_to(_bias_chain(d, tab), shape)


def _sample_bias(rel_bias_t):
    return pl.pallas_call(
        _sample_bias_kernel,
        out_shape=jax.ShapeDtypeStruct((2, PAGE_SIZE, N_HEADS, HEAD_DIM), F32),
        name="sample_bias",
    )(rel_bias_t)


def _topk_select(gates, k_sel):
    n = len(gates)
    if n <= k_sel:
        return [None] * n
    sel = []
    for j in range(n):
        cnt = jnp.zeros(gates[j].shape, F32)
        for j2 in range(n):
            if j2 == j:
                continue
            ahead = (gates[j2] >= gates[j]) if j2 < j else (gates[j2] > gates[j])
            cnt = cnt + jnp.where(ahead, 1.0, 0.0)
        sel.append(cnt < k_sel)
    return sel


def _prompt_attn_kernel(q_ref, k_ref, v_ref, bias_ref, o_ref, kb_scr, vt_scr, s_scr, p_scr):
    t = q_ref.shape[1]
    nb = t // MOBA_BLOCK
    heads = kb_scr.shape[0]
    nbuf = s_scr.shape[0] // heads
    nt = (((1,), (1,)), ((), ()))
    blk = lambda j: slice(j * MOBA_BLOCK, (j + 1) * MOBA_BLOCK)
    hcols = lambda hh: slice(hh * HEAD_DIM, (hh + 1) * HEAD_DIM)
    qs, masks, far_row = {}, {}, {}
    for hh in range(heads):
        k = k_ref[0, :, hcols(hh)]
        kb_scr[hh] = k.astype(BF16)
        vt_scr[hh, 0:HEAD_DIM, :] = v_ref[0, :, hcols(hh)].T.astype(BF16)
        vt_scr[hh, HEAD_DIM:, :] = jnp.ones((BF16_ROWS, t), BF16)
        ksum = jnp.sum(k.reshape(nb, MOBA_BLOCK, HEAD_DIM), axis=1)
        ks_hi = ksum.astype(BF16)
        ks_lo = (ksum - ks_hi.astype(F32)).astype(BF16)
        far_row[hh] = bias_ref[hh, 2][0:1, :]
        for i in range(nb):
            qs[hh, i] = q_ref[0, blk(i), hcols(hh)]
        for i in range(MOBA_TOPK + 1, nb):
            g = (lax.dot_general(ks_hi, qs[hh, i], nt, preferred_element_type=F32)
                 + lax.dot_general(ks_lo, qs[hh, i], nt, preferred_element_type=F32))
            sel = _topk_select([g[j:j + 1, :] for j in range(i)], MOBA_TOPK)
            masks[hh, i] = [jnp.where(sj, 0.0, NEG) for sj in sel]

    def biased_scores(hh, i):
        buf = hh * nbuf + i % nbuf
        nk = (i + 1) * MOBA_BLOCK
        sv = lax.dot_general(kb_scr[hh, 0:nk, :], qs[hh, i], nt, preferred_element_type=F32)
        m = None
        for j in range(i + 1):
            tile = sv[blk(j), :]
            mask = masks[hh, i][j] if ((hh, i) in masks and j < i) else None
            if j == i:
                tile = tile + bias_ref[hh, 0]
            elif j == i - 1:
                tile = tile + bias_ref[hh, 1]
                if mask is not None:
                    tile = tile + mask
            else:
                tile = tile + (far_row[hh] if mask is None else far_row[hh] + mask)
            s_scr[buf, blk(j), :] = tile
            cm = jnp.max(tile, axis=0, keepdims=True)
            m = cm if m is None else jnp.maximum(m, cm)
        return m

    col_max = {(hh, i): biased_scores(hh, i) for i in range(min(ATTN_LOOKAHEAD, nb)) for hh in range(heads)}
    for i in range(nb):
        for hh in range(heads):
            if i + ATTN_LOOKAHEAD < nb:
                col_max[hh, i + ATTN_LOOKAHEAD] = biased_scores(hh, i + ATTN_LOOKAHEAD)
            m = col_max.pop((hh, i))
            buf = hh * nbuf + i % nbuf
            nk = (i + 1) * MOBA_BLOCK
            for j in range(i + 1):
                p_scr[buf, blk(j), :] = jnp.exp2(s_scr[buf, blk(j), :] - m).astype(BF16)
            acc = jnp.dot(vt_scr[hh, :, 0:nk], p_scr[buf, 0:nk, :], preferred_element_type=F32)
            out = acc[0:HEAD_DIM, :] / acc[HEAD_DIM:HEAD_DIM + 1, :]
            o_ref[0, blk(i), hcols(hh)] = out.T.astype(o_ref.dtype)


def _prompt_attn(q, k, v, bias):
    b, t, _ = q.shape
    hps = ATTN_HEADS
    nbuf = ATTN_LOOKAHEAD + 1
    spec = pl.BlockSpec((1, t, hps * HEAD_DIM), lambda bi, hg: (bi, 0, hg))
    return pl.pallas_call(
        _prompt_attn_kernel, grid=(b, N_HEADS // hps),
        in_specs=[spec, spec, spec,
                  pl.BlockSpec((hps, 3, MOBA_BLOCK, MOBA_BLOCK), lambda bi, hg: (hg, 0, 0, 0))],
        out_specs=spec,
        out_shape=jax.ShapeDtypeStruct((b, t, N_HEADS * HEAD_DIM), BF16),
        scratch_shapes=[pltpu.VMEM((hps, t, HEAD_DIM), BF16), pltpu.VMEM((hps, HEAD_DIM + BF16_ROWS, t), BF16),
                        pltpu.VMEM((hps * nbuf, t, MOBA_BLOCK), F32),
                        pltpu.VMEM((hps * nbuf, t, MOBA_BLOCK), BF16)],
        compiler_params=_params("parallel", "arbitrary"),
        name="prompt_attn",
    )(q, k, v, bias)


SAMPLE_PAGES_PER_STEP = 8


def _sample_attn_kernel(pt_ref, q_ref, kn_ref, vn_ref, *refs):
    del pt_ref
    pps = SAMPLE_PAGES_PER_STEP
    k_refs, v_refs = refs[:pps], refs[pps:2 * pps]
    bias_ref, rbt_ref, o_ref, m_scr, l_scr, g_scr, o_scr = refs[2 * pps:]
    step = pl.program_id(1)
    n_pages = o_scr.shape[0]
    hd = (N_HEADS, HEAD_DIM)
    q = q_ref[0]

    last_step = step == pl.num_programs(1) - 1
    far = rbt_ref[:, N_BUCKETS - 1:N_BUCKETS] * LOG2E
    for r in range(pps):
        page = step * pps + r
        s = jnp.sum(k_refs[r][...] * q[None], axis=-1, keepdims=True)
        if r < pps - 1:
            logits = s
            m = jnp.max(s, axis=0)
            m_scr[page] = jnp.broadcast_to(m + far, hd)
        else:
            logits = s + bias_ref[jnp.where(last_step, 1, 0)]
            m = jnp.max(logits, axis=0)
            m_scr[page] = m
        e = jnp.exp2(logits - m[None])
        g_scr[page] = jnp.broadcast_to(jnp.sum(s, axis=0), hd)
        l_scr[page] = jnp.broadcast_to(jnp.sum(e, axis=0), hd)
        o_scr[page] = jnp.sum(e * v_refs[r][...], axis=0)

    @pl.when(last_step)
    def _():
        pages_per_block = MOBA_BLOCK // PAGE_SIZE
        n_blocks = n_pages // pages_per_block
        gates = [sum(g_scr[j * pages_per_block + r] for r in range(pages_per_block)) for j in range(n_blocks)]
        sel = _topk_select(gates, MOBA_TOPK)
        s_own = jnp.sum(q * kn_ref[0], axis=-1, keepdims=True) + rbt_ref[:, 0:1] * LOG2E
        m_tot = jnp.broadcast_to(s_own, hd)
        for j in range(n_blocks):
            for r in range(pages_per_block):
                mp = m_scr[j * pages_per_block + r]
                if sel[j] is not None:
                    mp = jnp.where(sel[j], mp, NEG)
                m_tot = jnp.maximum(m_tot, mp)
        w_own = jnp.exp2(s_own - m_tot)
        l_tot = w_own
        acc = w_own * vn_ref[0]
        for j in range(n_blocks):
            for r in range(pages_per_block):
                pg = j * pages_per_block + r
                w = jnp.exp2(m_scr[pg] - m_tot)
                if sel[j] is not None:
                    w = jnp.where(sel[j], w, 0.0)
                l_tot = l_tot + w * l_scr[pg]
                acc = acc + w * o_scr[pg]
        o_ref[0] = acc / l_tot


def _sample_attn(page_table, q, k_new, v_new, cache_k, cache_v, layer, bias, rel_bias_t):
    s, n_pages = page_table.shape
    pps = SAMPLE_PAGES_PER_STEP
    assert n_pages % pps == 0 and n_pages % (MOBA_BLOCK // PAGE_SIZE) == 0
    row = pl.BlockSpec((1, N_HEADS, HEAD_DIM), lambda si, st, pt: (si, 0, 0))

    def page(r):
        return pl.BlockSpec((None, None, PAGE_SIZE, N_HEADS, HEAD_DIM),
                            lambda si, st, pt: (layer, pt[si, st * pps + r], 0, 0, 0))

    stat = pltpu.VMEM((n_pages, N_HEADS, HEAD_DIM), F32)
    grid_spec = pltpu.PrefetchScalarGridSpec(
        num_scalar_prefetch=1, grid=(s, n_pages // pps),
        in_specs=[row, row, row] + [page(r) for r in range(pps)] * 2
        + [pl.BlockSpec((2, PAGE_SIZE, N_HEADS, HEAD_DIM), lambda si, st, pt: (0, 0, 0, 0)),
           pl.BlockSpec((N_HEADS, N_BUCKETS), lambda si, st, pt: (0, 0))],
        out_specs=row,
        scratch_shapes=[stat, stat, stat, stat],
    )
    return pl.pallas_call(
        _sample_attn_kernel, grid_spec=grid_spec,
        out_shape=jax.ShapeDtypeStruct((s, N_HEADS, HEAD_DIM), F32),
        compiler_params=_params("parallel", "arbitrary"),
        name="sample_attn",
    )(page_table, q, k_new, v_new, *([cache_k] * pps), *([cache_v] * pps), bias, rel_bias_t)


def _ln_swish(y, g, b):
    mu = jnp.mean(y, axis=-1, keepdims=True)
    yc = y - mu
    var = jnp.mean(yc * yc, axis=-1, keepdims=True)
    z = yc * lax.rsqrt(var + EPS) * g + b
    return z * jax.nn.sigmoid(z)


def _conv_prompt_kernel(cur_ref, halo_ref, w_ref, b_ref, g_ref, beta_ref, o_ref, xx_scr, pb_scr, y_scr):
    tt = cur_ref.shape[1]
    first = pl.program_id(1) == 0
    xx_scr[0:CONV_HALO, :] = jnp.where(first, 0.0, halo_ref[0])
    xx_scr[CONV_HALO:CONV_HALO + tt, :] = cur_ref[0]
    xx_scr[CONV_HALO + tt:, :] = jnp.zeros((SUBLANES, xx_scr.shape[1]), F32)
    lead = CONV_HALO - (CONV_K - 1)

    def body(cc, carry):
        cols = pl.ds(pl.multiple_of(cc * LANES, LANES), LANES)
        y = b_ref[:, cols]
        for b in range(SUBLANES):
            acc = None
            for a in range((lead + CONV_K - 1) // SUBLANES + 1):
                j = SUBLANES * a + b - lead
                if 0 <= j < CONV_K:
                    term = w_ref[j:j + 1, cols] * xx_scr[SUBLANES * a:SUBLANES * a + tt + SUBLANES, cols]
                    acc = term if acc is None else acc + term
            pb_scr[b] = acc
            y = y + pb_scr[b, b:b + tt, :]
        y_scr[:, cols] = y
        return carry

    lax.fori_loop(0, cur_ref.shape[2] // LANES, body, 0)
    o_ref[0] = _ln_swish(y_scr[...], g_ref[...], beta_ref[...]).astype(o_ref.dtype)


def _conv_prompt(u, w, b, g, beta, tt=256):
    bsz, t, c = u.shape
    vec = pl.BlockSpec((1, c), lambda bi, i: (0, 0))
    halo_blocks = tt // CONV_HALO
    return pl.pallas_call(
        _conv_prompt_kernel, grid=(bsz, t // tt),
        in_specs=[pl.BlockSpec((1, tt, c), lambda bi, i: (bi, i, 0)),
                  pl.BlockSpec((1, CONV_HALO, c), lambda bi, i: (bi, jnp.maximum(i * halo_blocks - 1, 0), 0)),
                  pl.BlockSpec((CONV_K, c), lambda bi, i: (0, 0)), vec, vec, vec],
        out_specs=pl.BlockSpec((1, tt, c), lambda bi, i: (bi, i, 0)),
        out_shape=jax.ShapeDtypeStruct((bsz, t, c), BF16),
        scratch_shapes=[pltpu.VMEM((CONV_HALO + tt + SUBLANES, c), F32),
                        pltpu.VMEM((SUBLANES, tt + SUBLANES, LANES), F32), pltpu.VMEM((tt, c), F32)],
        compiler_params=_params("parallel", "arbitrary"),
        name="conv_prompt",
    )(u, u, w, b, g, beta)


def _conv_sample_kernel(st_ref, u_ref, w_ref, b_ref, g_ref, beta_ref, y_ref, ns_ref):
    st = st_ref[...]
    u = u_ref[...]
    hist = CONV_K - 1
    y = jnp.sum(st * w_ref[0:hist, :][None], axis=1, keepdims=True) + u * w_ref[hist:CONV_K, :][None]
    y = y + b_ref[...][None]
    y_ref[...] = _ln_swish(y, g_ref[...][None], beta_ref[...][None])
    ns_ref[:, 0:hist - 1, :] = st_ref[:, 1:hist, :]
    ns_ref[:, hist - 1:hist, :] = u


def _conv_sample(state, u, w, b, g, beta, sb=16):
    s, hist, c = state.shape
    vec = pl.BlockSpec((1, c), lambda i: (0, 0))
    return pl.pallas_call(
        _conv_sample_kernel, grid=(s // sb,),
        in_specs=[pl.BlockSpec((sb, hist, c), lambda i: (i, 0, 0)),
                  pl.BlockSpec((sb, 1, c), lambda i: (i, 0, 0)),
                  pl.BlockSpec((CONV_K, c), lambda i: (0, 0)), vec, vec, vec],
        out_specs=[pl.BlockSpec((sb, 1, c), lambda i: (i, 0, 0)),
                   pl.BlockSpec((sb, hist, c), lambda i: (i, 0, 0))],
        out_shape=[jax.ShapeDtypeStruct((s, 1, c), F32), jax.ShapeDtypeStruct((s, hist, c), F32)],
        compiler_params=_params("parallel"),
        name="conv_sample",
    )(state, u, w, b, g, beta)


def _mix_kernel(x_ref, att_ref, yc_ref, h_ref, woa_ref, woc_ref, wga_ref, wgb_ref, wo_ref, o_ref, *copies):
    c = pl.program_id(1)
    woa_copy, woc_copy, wga_copy, wgb_copy, wo_copy = copies if copies else (None,) * 5

    def merged_chunk():
        h = h_ref[...]
        a_out = jnp.dot(att_ref[...], _bf16_weight(woa_ref, woa_copy), preferred_element_type=F32)
        c_out = jnp.dot(yc_ref[...], _bf16_weight(woc_ref, woc_copy), preferred_element_type=F32)
        ga = jnp.dot(h, _bf16_weight(wga_ref, wga_copy), preferred_element_type=F32)
        gb = jnp.dot(h, _bf16_weight(wgb_ref, wgb_copy), preferred_element_type=F32)
        merged = (jax.nn.sigmoid(ga) * a_out + jax.nn.sigmoid(gb) * c_out).astype(BF16)
        return jnp.dot(merged, _bf16_weight(wo_ref, wo_copy), preferred_element_type=F32)

    @pl.when(c == 0)
    def _():
        o_ref[...] = x_ref[...] + merged_chunk()

    @pl.when(c > 0)
    def _():
        o_ref[...] += merged_chunk()


def _mix(x, att, yc, h, w_out_att, w_out_conv, w_ga, w_gb, w_out, tm, copy_weights):
    m = x.shape[0]
    assert not copy_weights or m == tm
    tn = MIX_COPY_TN if copy_weights else MIX_TN
    row = pl.BlockSpec((tm, D_MODEL), lambda i, c: (i, 0))

    def wcol(c0):
        return pl.BlockSpec((D_MODEL, tn), lambda i, c: (0, c0 // tn + c))

    wrow = pl.BlockSpec((tn, D_MODEL), lambda i, c: (c, 0))
    out_shape = [jax.ShapeDtypeStruct((m, D_MODEL), F32)]
    out_specs = [row]
    if copy_weights:
        out_shape += [jax.ShapeDtypeStruct((D_MODEL, D_MODEL), BF16)] * 5
        out_specs += [wcol(0)] * 4 + [wrow]
    out = pl.pallas_call(
        _mix_kernel, grid=(m // tm, D_MODEL // tn),
        in_specs=[row, row, row, row, wcol(0), wcol(0), wcol(w_ga[1]), wcol(w_gb[1]), wrow],
        out_specs=out_specs, out_shape=out_shape,
        compiler_params=_params("parallel", "arbitrary"),
        name="mix",
    )(x, att, yc, h, w_out_att, w_out_conv, w_ga[0], w_gb[0], w_out)
    return out[0], (tuple(out[1:]) if copy_weights else None)


def _layer(x, tm, norms, w, attend_fn, conv_fn, q_dtype, copy_weights):
    cw = copy_weights
    x1, h, c_ffn1 = _ffn(x, norms["ffn1"], *w["ffn1"], norms["mix"], tm, cw)
    q, c_q = _proj("headnorm", h, [w["q"]], tm, q_dtype, gvec=norms["q"], scale=QK_SCALE, copy_weights=cw)
    k, c_k = _proj("headnorm", h, [w["k"]], tm, F32, gvec=norms["k"], copy_weights=cw)
    v, c_v = _proj("plain", h, [w["v"]], tm, F32, copy_weights=cw)
    u, c_u = _proj("glu", h, [w["a"], w["g"]], tm, F32, copy_weights=cw)
    att = attend_fn(q, k, v)
    yc, conv_state = conv_fn(u)
    x2, c_mix = _mix(x1, att, yc, h, w["out_att"], w["out_conv"], w["ga"], w["gb"], w["out"], tm, cw)
    y, _, c_ffn2 = _ffn(x2, norms["ffn2"], *w["ffn2"], None, tm, cw)
    copies = None
    if cw:
        copies = {"ffn1": c_ffn1, "ffn2": c_ffn2, "q": (c_q[0], 0), "k": (c_k[0], 0), "v": (c_v[0], 0),
                  "a": (c_u[0], 0), "g": (c_u[1], 0), "out_att": c_mix[0], "out_conv": c_mix[1],
                  "ga": (c_mix[2], 0), "gb": (c_mix[3], 0), "out": c_mix[4]}
    return y, k, v, conv_state, copies


def kernel(x_prompt, x_sample, cache_k, cache_v, state_conv, page_table, ffn1_norm, ffn1_w_gate, ffn1_w_up,
           ffn1_w_down, mix_norm, w_in, q_norm, k_norm, rel_bias, conv_dw_w, conv_dw_b, conv_ln_g, conv_ln_b,
           w_out_att, w_out_conv, w_out, ffn2_norm, ffn2_w_gate, ffn2_w_up, ffn2_w_down):
    depth = ffn1_norm.shape[0]
    assert depth == 1 and x_sample.shape[1] == 1
    bsz, seq, d = x_prompt.shape
    n_s = x_sample.shape[0]
    n_pages = page_table.shape[1]
    hist = CONV_K - 1
    l = 0
    norms = {"ffn1": ffn1_norm[l][None], "mix": mix_norm[l][None], "ffn2": ffn2_norm[l][None],
             "q": q_norm[l][None], "k": k_norm[l][None]}
    w_f32 = {"ffn1": (ffn1_w_gate[l], ffn1_w_up[l], ffn1_w_down[l]),
             "ffn2": (ffn2_w_gate[l], ffn2_w_up[l], ffn2_w_down[l]),
             "out_att": w_out_att[l], "out_conv": w_out_conv[l], "out": w_out[l]}
    for n, name in enumerate(("q", "k", "v", "a", "g", "ga", "gb")):
        w_f32[name] = (w_in[l], n * D_MODEL)
    dw_w, dw_b = conv_dw_w[l], conv_dw_b[l][None]
    ln_g, ln_b = conv_ln_g[l][None], conv_ln_b[l][None]

    assert n_pages * PAGE_SIZE >= MAX_DISTANCE + PAGE_SIZE
    rel_bias_t = rel_bias.T
    bias_s = _sample_bias(rel_bias_t)

    def attend_sample(q, k, v):
        shp = (n_s, N_HEADS, HEAD_DIM)
        att = _sample_attn(page_table, q.reshape(shp), k.reshape(shp), v.reshape(shp), cache_k, cache_v, l,
                           bias_s, rel_bias_t)
        return att.reshape(n_s, d).astype(BF16)

    def conv_sample(u):
        yc, ns = _conv_sample(state_conv[l], u.reshape(n_s, 1, d), dw_w, dw_b, ln_g, ln_b)
        return yc.reshape(n_s, d).astype(BF16), ns

    ys, ks, vs, cs, w_bf16 = _layer(x_sample.reshape(n_s, d), n_s, norms, w_f32, attend_sample, conv_sample, F32,
                                    copy_weights=True)

    bias_p = _prompt_bias(rel_bias)

    def attend_prompt(q, k, v):
        shp = (bsz, seq, d)
        return _prompt_attn(q.reshape(shp), k.reshape(shp), v.reshape(shp), bias_p).reshape(bsz * seq, d)

    def conv_prompt(u):
        u3 = u.reshape(bsz, seq, d)
        yc = _conv_prompt(u3, dw_w, dw_b, ln_g, ln_b).reshape(bsz * seq, d)
        return yc, u3[:, seq - hist:, :]

    yp, kp, vp, cp, _ = _layer(x_prompt.reshape(bsz * seq, d), 512, norms, w_bf16, attend_prompt, conv_prompt, BF16,
                               copy_weights=False)

    heads = (N_HEADS, HEAD_DIM)
    return (yp.reshape(bsz, seq, d), ys.reshape(n_s, 1, d),
            kp.reshape(1, bsz, seq, *heads), vp.reshape(1, bsz, seq, *heads), cp[None],
            ks.reshape(1, n_s, 1, *heads), vs.reshape(1, n_s, 1, *heads), cs[None])
```

```python
import functools
import math

import numpy as np
import jax
import jax.numpy as jnp
from jax import lax
from jax.experimental import pallas as pl
from jax.experimental.pallas import tpu as pltpu

D_MODEL = 2048
N_HEADS = 16
HEAD_DIM = 128
D_FF = 5632
CONV_K = 31
MOBA_BLOCK = 256
MOBA_TOPK = 3
N_BUCKETS = 32
MAX_DISTANCE = 128
PAGE_SIZE = 128
EPS = 1e-6
NEG = -1e30
SCALE = HEAD_DIM ** -0.5
LOG2E = math.log2(math.e)
QK_SCALE = SCALE * LOG2E

F32 = jnp.float32
BF16 = jnp.bfloat16

VMEM_LIMIT_BYTES = 56 * 1024 * 1024
LANES = 128
SUBLANES = 8
BF16_ROWS = 16
CONV_HALO = 32
FFN_TF = 512
MIX_TN = 512
ATTN_LOOKAHEAD = 1
ATTN_HEADS = 2
PROJ_COPY_TN = 512
MIX_COPY_TN = 256


def _params(*sem):
    return pltpu.CompilerParams(dimension_semantics=sem, vmem_limit_bytes=VMEM_LIMIT_BYTES)


def _t5_bucket_np(n, dtype):
    n = np.maximum(n, 0)
    max_exact = N_BUCKETS // 2
    ratio = np.log(np.maximum(n, 1).astype(dtype) / dtype(max_exact)) / dtype(math.log(MAX_DISTANCE / max_exact))
    large = max_exact + (ratio * dtype(N_BUCKETS - max_exact)).astype(np.int32)
    large = np.minimum(large, N_BUCKETS - 1)
    return np.where(n < max_exact, n, large)


def _bucket_thresholds():
    d = np.arange(0, 8192)
    b32 = _t5_bucket_np(d, np.float32)
    b64 = _t5_bucket_np(d, np.float64)
    assert np.array_equal(b32, b64) and np.all(np.diff(b32) >= 0)
    assert np.all(b32[MAX_DISTANCE:] == N_BUCKETS - 1)
    return [int(np.argmax(b32 >= t)) for t in range(N_BUCKETS)]


BUCKET_THR = _bucket_thresholds()


def _rms(x, g):
    ms = jnp.mean(x * x, axis=-1, keepdims=True)
    return x * lax.rsqrt(ms + EPS) * g


def _bf16_weight(w_ref, copy_ref):
    w = w_ref[...]
    if copy_ref is None:
        return w
    w = w.astype(BF16)
    copy_ref[...] = w
    return w


def _ffn_kernel(*refs, emit_norm, copy_weights):
    refs = list(refs)
    x_ref, g_ref, wg_ref, wu_ref, wd_ref = refs[:5]
    del refs[:5]
    gn_ref = refs.pop(0) if emit_norm else None
    o_ref = refs.pop(0)
    hn_ref = refs.pop(0) if emit_norm else None
    wg_copy, wu_copy, wd_copy = (refs.pop(0), refs.pop(0), refs.pop(0)) if copy_weights else (None, None, None)
    h_scr, = refs
    f = pl.program_id(1)
    last = pl.num_programs(1) - 1

    def hidden_chunk(h):
        gate = jnp.dot(h, _bf16_weight(wg_ref, wg_copy), preferred_element_type=F32)
        up = jnp.dot(h, _bf16_weight(wu_ref, wu_copy), preferred_element_type=F32)
        hid = (gate * jax.nn.sigmoid(gate) * up).astype(BF16)
        return jnp.dot(hid, _bf16_weight(wd_ref, wd_copy), preferred_element_type=F32)

    @pl.when(f == 0)
    def _():
        h = _rms(x_ref[...], g_ref[...]).astype(BF16)
        h_scr[...] = h
        o_ref[...] = hidden_chunk(h)

    @pl.when(jnp.logical_and(f > 0, f < last))
    def _():
        o_ref[...] += hidden_chunk(h_scr[...])

    @pl.when(f == last)
    def _():
        y = x_ref[...] + 0.5 * (o_ref[...] + hidden_chunk(h_scr[...]))
        o_ref[...] = y
        if emit_norm:
            hn_ref[...] = _rms(y, gn_ref[...]).astype(BF16)


def _ffn(x, g, wg, wu, wd, gn, tm, copy_weights):
    m = x.shape[0]
    tf = FFN_TF
    assert D_FF // tf >= 2
    assert not copy_weights or m == tm
    emit_norm = gn is not None
    row = pl.BlockSpec((tm, D_MODEL), lambda i, f: (i, 0))
    vec = pl.BlockSpec((1, D_MODEL), lambda i, f: (0, 0))
    wspecs = [pl.BlockSpec((D_MODEL, tf), lambda i, f: (0, f)),
              pl.BlockSpec((D_MODEL, tf), lambda i, f: (0, f)),
              pl.BlockSpec((tf, D_MODEL), lambda i, f: (f, 0))]
    in_specs = [row, vec] + wspecs
    args = [x, g, wg, wu, wd]
    out_shape = [jax.ShapeDtypeStruct((m, D_MODEL), F32)]
    out_specs = [row]
    if emit_norm:
        in_specs.append(vec)
        args.append(gn)
        out_shape.append(jax.ShapeDtypeStruct((m, D_MODEL), BF16))
        out_specs.append(row)
    if copy_weights:
        out_shape += [jax.ShapeDtypeStruct(w.shape, BF16) for w in (wg, wu, wd)]
        out_specs += wspecs
    out = list(pl.pallas_call(
        functools.partial(_ffn_kernel, emit_norm=emit_norm, copy_weights=copy_weights),
        grid=(m // tm, D_FF // tf),
        in_specs=in_specs, out_specs=out_specs, out_shape=out_shape,
        scratch_shapes=[pltpu.VMEM((tm, D_MODEL), BF16)],
        compiler_params=_params("parallel", "arbitrary"),
        name="ffn",
    )(*args))
    y = out.pop(0)
    hn = out.pop(0) if emit_norm else None
    return y, hn, (tuple(out) if copy_weights else None)


def _proj_headnorm_kernel(h_ref, w_ref, g_ref, o_ref, w_copy=None, *, scale):
    acc = jnp.dot(h_ref[...], _bf16_weight(w_ref, w_copy), preferred_element_type=F32)
    g = g_ref[...] * scale
    for s in range(acc.shape[1] // HEAD_DIM):
        sl = slice(s * HEAD_DIM, (s + 1) * HEAD_DIM)
        o_ref[:, sl] = _rms(acc[:, sl], g).astype(o_ref.dtype)


def _proj_plain_kernel(h_ref, w_ref, o_ref, w_copy=None):
    o_ref[...] = jnp.dot(h_ref[...], _bf16_weight(w_ref, w_copy), preferred_element_type=F32).astype(o_ref.dtype)


def _proj_glu_kernel(h_ref, wa_ref, wg_ref, o_ref, wa_copy=None, wg_copy=None):
    h = h_ref[...]
    a = jnp.dot(h, _bf16_weight(wa_ref, wa_copy), preferred_element_type=F32)
    g = jnp.dot(h, _bf16_weight(wg_ref, wg_copy), preferred_element_type=F32)
    o_ref[...] = a * jax.nn.sigmoid(g)


def _proj(kind, h, weights, tm, out_dtype, gvec=None, scale=1.0, copy_weights=False):
    m = h.shape[0]
    assert not copy_weights or m == tm
    if copy_weights:
        tn = PROJ_COPY_TN
    else:
        tn = D_MODEL // 2 if kind == "glu" else D_MODEL
    hspec = pl.BlockSpec((tm, D_MODEL), lambda i, j: (i, 0))
    ospec = pl.BlockSpec((tm, tn), lambda i, j: (i, j))

    def wspec(c0):
        return pl.BlockSpec((D_MODEL, tn), lambda i, j: (0, c0 // tn + j))

    in_specs = [hspec] + [wspec(c0) for _, c0 in weights]
    args = [h] + [w for w, _ in weights]
    if kind == "headnorm":
        body = functools.partial(_proj_headnorm_kernel, scale=scale)
        in_specs.append(pl.BlockSpec((1, HEAD_DIM), lambda i, j: (0, 0)))
        args.append(gvec)
    else:
        body = _proj_glu_kernel if kind == "glu" else _proj_plain_kernel
    out_shape = [jax.ShapeDtypeStruct((m, D_MODEL), out_dtype)]
    out_specs = [ospec]
    if copy_weights:
        out_shape += [jax.ShapeDtypeStruct((D_MODEL, D_MODEL), BF16)] * len(weights)
        out_specs += [wspec(0)] * len(weights)
    out = pl.pallas_call(
        body, grid=(m // tm, D_MODEL // tn), in_specs=in_specs, out_specs=out_specs, out_shape=out_shape,
        compiler_params=_params("parallel", "arbitrary"),
        name="proj_" + kind,
    )(*args)
    return out[0], (tuple(out[1:]) if copy_weights else None)


def _bias_chain(d, tab):
    val = tab(0)
    for t in range(1, N_BUCKETS):
        val = jnp.where(d >= BUCKET_THR[t], tab(t), val)
    return val


def _prompt_bias_kernel(rb_ref, o_ref):
    h = pl.program_id(0)
    shape = (MOBA_BLOCK, MOBA_BLOCK)
    d = lax.broadcasted_iota(jnp.int32, shape, 1) - lax.broadcasted_iota(jnp.int32, shape, 0)
    tab = lambda t: rb_ref[t, h] * LOG2E
    own = _bias_chain(jnp.maximum(d, 0), tab)
    o_ref[0, 0] = jnp.where(d >= 0, own, NEG)
    o_ref[0, 1] = _bias_chain(d + MOBA_BLOCK, tab)
    o_ref[0, 2] = jnp.full(shape, tab(N_BUCKETS - 1), F32)


def _prompt_bias(rel_bias):
    return pl.pallas_call(
        _prompt_bias_kernel, grid=(N_HEADS,),
        in_specs=[pl.BlockSpec(memory_space=pltpu.SMEM)],
        out_specs=pl.BlockSpec((1, 3, MOBA_BLOCK, MOBA_BLOCK), lambda h: (h, 0, 0, 0)),
        out_shape=jax.ShapeDtypeStruct((N_HEADS, 3, MOBA_BLOCK, MOBA_BLOCK), F32),
        compiler_params=_params("arbitrary"),
        name="prompt_bias",
    )(rel_bias)


def _sample_bias_kernel(rbt_ref, o_ref):
    shape = (PAGE_SIZE, N_HEADS, HEAD_DIM)
    tab = lambda t: rbt_ref[:, t:t + 1][None] * LOG2E
    d = PAGE_SIZE - lax.broadcasted_iota(jnp.int32, shape, 0)
    o_ref[0] = jnp.broadcast_to(tab(N_BUCKETS - 1), shape)
    o_ref[1] = jnp.broadcast_to(_bias_chain(d, tab), shape)


def _sample_bias(rel_bias_t):
    return pl.pallas_call(
        _sample_bias_kernel,
        out_shape=jax.ShapeDtypeStruct((2, PAGE_SIZE, N_HEADS, HEAD_DIM), F32),
        name="sample_bias",
    )(rel_bias_t)


def _topk_select(gates, k_sel):
    n = len(gates)
    if n <= k_sel:
        return [None] * n
    sel = []
    for j in range(n):
        cnt = jnp.zeros(gates[j].shape, F32)
        for j2 in range(n):
            if j2 == j:
                continue
            ahead = (gates[j2] >= gates[j]) if j2 < j else (gates[j2] > gates[j])
            cnt = cnt + jnp.where(ahead, 1.0, 0.0)
        sel.append(cnt < k_sel)
    return sel


def _prompt_attn_kernel(q_ref, k_ref, v_ref, bias_ref, o_ref, kb_scr, vt_scr, s_scr, p_scr):
    t = q_ref.shape[1]
    nb = t // MOBA_BLOCK
    heads = kb_scr.shape[0]
    nbuf = s_scr.shape[0] // heads
    nt = (((1,), (1,)), ((), ()))
    blk = lambda j: slice(j * MOBA_BLOCK, (j + 1) * MOBA_BLOCK)
    hcols = lambda hh: slice(hh * HEAD_DIM, (hh + 1) * HEAD_DIM)
    qs, masks, far_row = {}, {}, {}
    for hh in range(heads):
        k = k_ref[0, :, hcols(hh)]
        kb_scr[hh] = k.astype(BF16)
        vt_scr[hh, 0:HEAD_DIM, :] = v_ref[0, :, hcols(hh)].T.astype(BF16)
        vt_scr[hh, HEAD_DIM:, :] = jnp.ones((BF16_ROWS, t), BF16)
        ksum = jnp.sum(k.reshape(nb, MOBA_BLOCK, HEAD_DIM), axis=1)
        ks_hi = ksum.astype(BF16)
        ks_lo = (ksum - ks_hi.astype(F32)).astype(BF16)
        far_row[hh] = bias_ref[hh, 2][0:1, :]
        for i in range(nb):
            qs[hh, i] = q_ref[0, blk(i), hcols(hh)]
        for i in range(MOBA_TOPK + 1, nb):
            g = (lax.dot_general(ks_hi, qs[hh, i], nt, preferred_element_type=F32)
                 + lax.dot_general(ks_lo, qs[hh, i], nt, preferred_element_type=F32))
            sel = _topk_select([g[j:j + 1, :] for j in range(i)], MOBA_TOPK)
            masks[hh, i] = [jnp.where(sj, 0.0, NEG) for sj in sel]

    def biased_scores(hh, i):
        buf = hh * nbuf + i % nbuf
        nk = (i + 1) * MOBA_BLOCK
        sv = lax.dot_general(kb_scr[hh, 0:nk, :], qs[hh, i], nt, preferred_element_type=F32)
        m = None
        for j in range(i + 1):
            tile = sv[blk(j), :]
            mask = masks[hh, i][j] if ((hh, i) in masks and j < i) else None
            if j == i:
                tile = tile + bias_ref[hh, 0]
            elif j == i - 1:
                tile = tile + bias_ref[hh, 1]
                if mask is not None:
                    tile = tile + mask
            else:
                tile = tile + (far_row[hh] if mask is None else far_row[hh] + mask)
            s_scr[buf, blk(j), :] = tile
            cm = jnp.max(tile, axis=0, keepdims=True)
            m = cm if m is None else jnp.maximum(m, cm)
        return m

    col_max = {(hh, i): biased_scores(hh, i) for i in range(min(ATTN_LOOKAHEAD, nb)) for hh in range(heads)}
    for i in range(nb):
        for hh in range(heads):
            if i + ATTN_LOOKAHEAD < nb:
                col_max[hh, i + ATTN_LOOKAHEAD] = biased_scores(hh, i + ATTN_LOOKAHEAD)
            m = col_max.pop((hh, i))
            buf = hh * nbuf + i % nbuf
            nk = (i + 1) * MOBA_BLOCK
            for j in range(i + 1):
                p_scr[buf, blk(j), :] = jnp.exp2(s_scr[buf, blk(j), :] - m).astype(BF16)
            acc = jnp.dot(vt_scr[hh, :, 0:nk], p_scr[buf, 0:nk, :], preferred_element_type=F32)
            out = acc[0:HEAD_DIM, :] / acc[HEAD_DIM:HEAD_DIM + 1, :]
            o_ref[0, blk(i), hcols(hh)] = out.T.astype(o_ref.dtype)


def _prompt_attn(q, k, v, bias):
    b, t, _ = q.shape
    hps = ATTN_HEADS
    nbuf = ATTN_LOOKAHEAD + 1
    spec = pl.BlockSpec((1, t, hps * HEAD_DIM), lambda bi, hg: (bi, 0, hg))
    return pl.pallas_call(
        _prompt_attn_kernel, grid=(b, N_HEADS // hps),
        in_specs=[spec, spec, spec,
                  pl.BlockSpec((hps, 3, MOBA_BLOCK, MOBA_BLOCK), lambda bi, hg: (hg, 0, 0, 0))],
        out_specs=spec,
        out_shape=jax.ShapeDtypeStruct((b, t, N_HEADS * HEAD_DIM), BF16),
        scratch_shapes=[pltpu.VMEM((hps, t, HEAD_DIM), BF16), pltpu.VMEM((hps, HEAD_DIM + BF16_ROWS, t), BF16),
                        pltpu.VMEM((hps * nbuf, t, MOBA_BLOCK), F32),
                        pltpu.VMEM((hps * nbuf, t, MOBA_BLOCK), BF16)],
        compiler_params=_params("parallel", "arbitrary"),
        name="prompt_attn",
    )(q, k, v, bias)


SAMPLE_PAGES_PER_STEP = 8


def _sample_attn_kernel(pt_ref, q_ref, kn_ref, vn_ref, *refs):
    del pt_ref
    pps = SAMPLE_PAGES_PER_STEP
    k_refs, v_refs = refs[:pps], refs[pps:2 * pps]
    bias_ref, rbt_ref, o_ref, m_scr, l_scr, g_scr, o_scr = refs[2 * pps:]
    step = pl.program_id(1)
    n_pages = o_scr.shape[0]
    hd = (N_HEADS, HEAD_DIM)
    q = q_ref[0]

    last_step = step == pl.num_programs(1) - 1
    far = rbt_ref[:, N_BUCKETS - 1:N_BUCKETS] * LOG2E
    for r in range(pps):
        page = step * pps + r
        for h0 in range(0, N_HEADS, SUBLANES):
            hs = slice(h0, h0 + SUBLANES)
            hg = (SUBLANES, HEAD_DIM)
            s = jnp.sum(k_refs[r][:, hs, :] * q[hs][None], axis=-1, keepdims=True)
            if r < pps - 1:
                logits = s
                m = jnp.max(s, axis=0)
                m_scr[page, hs, :] = jnp.broadcast_to(m + far[hs], hg)
            else:
                logits = s + bias_ref[jnp.where(last_step, 1, 0), :, hs, :]
                m = jnp.max(logits, axis=0)
                m_scr[page, hs, :] = m
            e = jnp.exp2(logits - m[None])
            g_scr[page, hs, :] = jnp.broadcast_to(jnp.sum(s, axis=0), hg)
            l_scr[page, hs, :] = jnp.broadcast_to(jnp.sum(e, axis=0), hg)
            o_scr[page, hs, :] = jnp.sum(e * v_refs[r][:, hs, :], axis=0)

    @pl.when(last_step)
    def _():
        pages_per_block = MOBA_BLOCK // PAGE_SIZE
        n_blocks = n_pages // pages_per_block
        gates = [sum(g_scr[j * pages_per_block + r] for r in range(pages_per_block)) for j in range(n_blocks)]
        sel = _topk_select(gates, MOBA_TOPK)
        s_own = jnp.sum(q * kn_ref[0], axis=-1, keepdims=True) + rbt_ref[:, 0:1] * LOG2E
        m_tot = jnp.broadcast_to(s_own, hd)
        for j in range(n_blocks):
            for r in range(pages_per_block):
                mp = m_scr[j * pages_per_block + r]
                if sel[j] is not None:
                    mp = jnp.where(sel[j], mp, NEG)
                m_tot = jnp.maximum(m_tot, mp)
        w_own = jnp.exp2(s_own - m_tot)
        l_tot = w_own
        acc = w_own * vn_ref[0]
        for j in range(n_blocks):
            for r in range(pages_per_block):
                pg = j * pages_per_block + r
                w = jnp.exp2(m_scr[pg] - m_tot)
                if sel[j] is not None:
                    w = jnp.where(sel[j], w, 0.0)
                l_tot = l_tot + w * l_scr[pg]
                acc = acc + w * o_scr[pg]
        o_ref[0] = acc / l_tot


def _sample_attn(page_table, q, k_new, v_new, cache_k, cache_v, layer, bias, rel_bias_t):
    s, n_pages = page_table.shape
    pps = SAMPLE_PAGES_PER_STEP
    assert n_pages % pps == 0 and n_pages % (MOBA_BLOCK // PAGE_SIZE) == 0
    row = pl.BlockSpec((1, N_HEADS, HEAD_DIM), lambda si, st, pt: (si, 0, 0))

    def page(r):
        return pl.BlockSpec((None, None, PAGE_SIZE, N_HEADS, HEAD_DIM),
                            lambda si, st, pt: (layer, pt[si, st * pps + r], 0, 0, 0))

    stat = pltpu.VMEM((n_pages, N_HEADS, HEAD_DIM), F32)
    grid_spec = pltpu.PrefetchScalarGridSpec(
        num_scalar_prefetch=1, grid=(s, n_pages // pps),
        in_specs=[row, row, row] + [page(r) for r in range(pps)] * 2
        + [pl.BlockSpec((2, PAGE_SIZE, N_HEADS, HEAD_DIM), lambda si, st, pt: (0, 0, 0, 0)),
           pl.BlockSpec((N_HEADS, N_BUCKETS), lambda si, st, pt: (0, 0))],
        out_specs=row,
        scratch_shapes=[stat, stat, stat, stat],
    )
    return pl.pallas_call(
        _sample_attn_kernel, grid_spec=grid_spec,
        out_shape=jax.ShapeDtypeStruct((s, N_HEADS, HEAD_DIM), F32),
        compiler_params=_params("parallel", "arbitrary"),
        name="sample_attn",
    )(page_table, q, k_new, v_new, *([cache_k] * pps), *([cache_v] * pps), bias, rel_bias_t)


def _ln_swish(y, g, b):
    mu = jnp.mean(y, axis=-1, keepdims=True)
    yc = y - mu
    var = jnp.mean(yc * yc, axis=-1, keepdims=True)
    z = yc * lax.rsqrt(var + EPS) * g + b
    return z * jax.nn.sigmoid(z)


def _conv_prompt_kernel(cur_ref, halo_ref, w_ref, b_ref, g_ref, beta_ref, o_ref, xx_scr, pb_scr, y_scr):
    tt = cur_ref.shape[1]
    first = pl.program_id(1) == 0
    xx_scr[0:CONV_HALO, :] = jnp.where(first, 0.0, halo_ref[0])
    xx_scr[CONV_HALO:CONV_HALO + tt, :] = cur_ref[0]
    xx_scr[CONV_HALO + tt:, :] = jnp.zeros((SUBLANES, xx_scr.shape[1]), F32)
    lead = CONV_HALO - (CONV_K - 1)

    def body(cc, carry):
        cols = pl.ds(pl.multiple_of(cc * LANES, LANES), LANES)
        y = b_ref[:, cols]
        for b in range(SUBLANES):
            acc = None
            for a in range((lead + CONV_K - 1) // SUBLANES + 1):
                j = SUBLANES * a + b - lead
                if 0 <= j < CONV_K:
                    term = w_ref[j:j + 1, cols] * xx_scr[SUBLANES * a:SUBLANES * a + tt + SUBLANES, cols]
                    acc = term if acc is None else acc + term
            pb_scr[b] = acc
            y = y + pb_scr[b, b:b + tt, :]
        y_scr[:, cols] = y
        return carry

    lax.fori_loop(0, cur_ref.shape[2] // LANES, body, 0)
    o_ref[0] = _ln_swish(y_scr[...], g_ref[...], beta_ref[...]).astype(o_ref.dtype)


def _conv_prompt(u, w, b, g, beta, tt=256):
    bsz, t, c = u.shape
    vec = pl.BlockSpec((1, c), lambda bi, i: (0, 0))
    halo_blocks = tt // CONV_HALO
    return pl.pallas_call(
        _conv_prompt_kernel, grid=(bsz, t // tt),
        in_specs=[pl.BlockSpec((1, tt, c), lambda bi, i: (bi, i, 0)),
                  pl.BlockSpec((1, CONV_HALO, c), lambda bi, i: (bi, jnp.maximum(i * halo_blocks - 1, 0), 0)),
                  pl.BlockSpec((CONV_K, c), lambda bi, i: (0, 0)), vec, vec, vec],
        out_specs=pl.BlockSpec((1, tt, c), lambda bi, i: (bi, i, 0)),
        out_shape=jax.ShapeDtypeStruct((bsz, t, c), BF16),
        scratch_shapes=[pltpu.VMEM((CONV_HALO + tt + SUBLANES, c), F32),
                        pltpu.VMEM((SUBLANES, tt + SUBLANES, LANES), F32), pltpu.VMEM((tt, c), F32)],
        compiler_params=_params("parallel", "arbitrary"),
        name="conv_prompt",
    )(u, u, w, b, g, beta)


def _conv_sample_kernel(st_ref, u_ref, w_ref, b_ref, g_ref, beta_ref, y_ref, ns_ref):
    st = st_ref[...]
    u = u_ref[...]
    hist = CONV_K - 1
    y = jnp.sum(st * w_ref[0:hist, :][None], axis=1, keepdims=True) + u * w_ref[hist:CONV_K, :][None]
    y = y + b_ref[...][None]
    y_ref[...] = _ln_swish(y, g_ref[...][None], beta_ref[...][None])
    ns_ref[:, 0:hist - 1, :] = st_ref[:, 1:hist, :]
    ns_ref[:, hist - 1:hist, :] = u


def _conv_sample(state, u, w, b, g, beta, sb=16):
    s, hist, c = state.shape
    vec = pl.BlockSpec((1, c), lambda i: (0, 0))
    return pl.pallas_call(
        _conv_sample_kernel, grid=(s // sb,),
        in_specs=[pl.BlockSpec((sb, hist, c), lambda i: (i, 0, 0)),
                  pl.BlockSpec((sb, 1, c), lambda i: (i, 0, 0)),
                  pl.BlockSpec((CONV_K, c), lambda i: (0, 0)), vec, vec, vec],
        out_specs=[pl.BlockSpec((sb, 1, c), lambda i: (i, 0, 0)),
                   pl.BlockSpec((sb, hist, c), lambda i: (i, 0, 0))],
        out_shape=[jax.ShapeDtypeStruct((s, 1, c), F32), jax.ShapeDtypeStruct((s, hist, c), F32)],
        compiler_params=_params("parallel"),
        name="conv_sample",
    )(state, u, w, b, g, beta)


def _mix_kernel(x_ref, att_ref, yc_ref, h_ref, woa_ref, woc_ref, wga_ref, wgb_ref, wo_ref, o_ref, *copies):
    c = pl.program_id(1)
    woa_copy, woc_copy, wga_copy, wgb_copy, wo_copy = copies if copies else (None,) * 5

    def merged_chunk():
        h = h_ref[...]
        a_out = jnp.dot(att_ref[...], _bf16_weight(woa_ref, woa_copy), preferred_element_type=F32)
        c_out = jnp.dot(yc_ref[...], _bf16_weight(woc_ref, woc_copy), preferred_element_type=F32)
        ga = jnp.dot(h, _bf16_weight(wga_ref, wga_copy), preferred_element_type=F32)
        gb = jnp.dot(h, _bf16_weight(wgb_ref, wgb_copy), preferred_element_type=F32)
        merged = (jax.nn.sigmoid(ga) * a_out + jax.nn.sigmoid(gb) * c_out).astype(BF16)
        return jnp.dot(merged, _bf16_weight(wo_ref, wo_copy), preferred_element_type=F32)

    @pl.when(c == 0)
    def _():
        o_ref[...] = x_ref[...] + merged_chunk()

    @pl.when(c > 0)
    def _():
        o_ref[...] += merged_chunk()


def _mix(x, att, yc, h, w_out_att, w_out_conv, w_ga, w_gb, w_out, tm, copy_weights):
    m = x.shape[0]
    assert not copy_weights or m == tm
    tn = MIX_COPY_TN if copy_weights else MIX_TN
    row = pl.BlockSpec((tm, D_MODEL), lambda i, c: (i, 0))

    def wcol(c0):
        return pl.BlockSpec((D_MODEL, tn), lambda i, c: (0, c0 // tn + c))

    wrow = pl.BlockSpec((tn, D_MODEL), lambda i, c: (c, 0))
    out_shape = [jax.ShapeDtypeStruct((m, D_MODEL), F32)]
    out_specs = [row]
    if copy_weights:
        out_shape += [jax.ShapeDtypeStruct((D_MODEL, D_MODEL), BF16)] * 5
        out_specs += [wcol(0)] * 4 + [wrow]
    out = pl.pallas_call(
        _mix_kernel, grid=(m // tm, D_MODEL // tn),
        in_specs=[row, row, row, row, wcol(0), wcol(0), wcol(w_ga[1]), wcol(w_gb[1]), wrow],
        out_specs=out_specs, out_shape=out_shape,
        compiler_params=_params("parallel", "arbitrary"),
        name="mix",
    )(x, att, yc, h, w_out_att, w_out_conv, w_ga[0], w_gb[0], w_out)
    return out[0], (tuple(out[1:]) if copy_weights else None)


def _layer(x, tm, norms, w, attend_fn, conv_fn, q_dtype, copy_weights):
    cw = copy_weights
    x1, h, c_ffn1 = _ffn(x, norms["ffn1"], *w["ffn1"], norms["mix"], tm, cw)
    q, c_q = _proj("headnorm", h, [w["q"]], tm, q_dtype, gvec=norms["q"], scale=QK_SCALE, copy_weights=cw)
    k, c_k = _proj("headnorm", h, [w["k"]], tm, F32, gvec=norms["k"], copy_weights=cw)
    v, c_v = _proj("plain", h, [w["v"]], tm, F32, copy_weights=cw)
    u, c_u = _proj("glu", h, [w["a"], w["g"]], tm, F32, copy_weights=cw)
    att = attend_fn(q, k, v)
    yc, conv_state = conv_fn(u)
    x2, c_mix = _mix(x1, att, yc, h, w["out_att"], w["out_conv"], w["ga"], w["gb"], w["out"], tm, cw)
    y, _, c_ffn2 = _ffn(x2, norms["ffn2"], *w["ffn2"], None, tm, cw)
    copies = None
    if cw:
        copies = {"ffn1": c_ffn1, "ffn2": c_ffn2, "q": (c_q[0], 0), "k": (c_k[0], 0), "v": (c_v[0], 0),
                  "a": (c_u[0], 0), "g": (c_u[1], 0), "out_att": c_mix[0], "out_conv": c_mix[1],
                  "ga": (c_mix[2], 0), "gb": (c_mix[3], 0), "out": c_mix[4]}
    return y, k, v, conv_state, copies


def kernel(x_prompt, x_sample, cache_k, cache_v, state_conv, page_table, ffn1_norm, ffn1_w_gate, ffn1_w_up,
           ffn1_w_down, mix_norm, w_in, q_norm, k_norm, rel_bias, conv_dw_w, conv_dw_b, conv_ln_g, conv_ln_b,
           w_out_att, w_out_conv, w_out, ffn2_norm, ffn2_w_gate, ffn2_w_up, ffn2_w_down):
    depth = ffn1_norm.shape[0]
    assert depth == 1 and x_sample.shape[1] == 1
    bsz, seq, d = x_prompt.shape
    n_s = x_sample.shape[0]
    n_pages = page_table.shape[1]
    hist = CONV_K - 1
    l = 0
    norms = {"ffn1": ffn1_norm[l][None], "mix": mix_norm[l][None], "ffn2": ffn2_norm[l][None],
             "q": q_norm[l][None], "k": k_norm[l][None]}
    w_f32 = {"ffn1": (ffn1_w_gate[l], ffn1_w_up[l], ffn1_w_down[l]),
             "ffn2": (ffn2_w_gate[l], ffn2_w_up[l], ffn2_w_down[l]),
             "out_att": w_out_att[l], "out_conv": w_out_conv[l], "out": w_out[l]}
    for n, name in enumerate(("q", "k", "v", "a", "g", "ga", "gb")):
        w_f32[name] = (w_in[l], n * D_MODEL)
    dw_w, dw_b = conv_dw_w[l], conv_dw_b[l][None]
    ln_g, ln_b = conv_ln_g[l][None], conv_ln_b[l][None]

    assert n_pages * PAGE_SIZE >= MAX_DISTANCE + PAGE_SIZE
    rel_bias_t = rel_bias.T
    bias_s = _sample_bias(rel_bias_t)

    def attend_sample(q, k, v):
        shp = (n_s, N_HEADS, HEAD_DIM)
        att = _sample_attn(page_table, q.reshape(shp), k.reshape(shp), v.reshape(shp), cache_k, cache_v, l,
                           bias_s, rel_bias_t)
        return att.reshape(n_s, d).astype(BF16)

    def conv_sample(u):
        yc, ns = _conv_sample(state_conv[l], u.reshape(n_s, 1, d), dw_w, dw_b, ln_g, ln_b)
        return yc.reshape(n_s, d).astype(BF16), ns

    ys, ks, vs, cs, w_bf16 = _layer(x_sample.reshape(n_s, d), n_s, norms, w_f32, attend_sample, conv_sample, F32,
                                    copy_weights=True)

    bias_p = _prompt_bias(rel_bias)

    def attend_prompt(q, k, v):
        shp = (bsz, seq, d)
        return _prompt_attn(q.reshape(shp), k.reshape(shp), v.reshape(shp), bias_p).reshape(bsz * seq, d)

    def conv_prompt(u):
        u3 = u.reshape(bsz, seq, d)
        yc = _conv_prompt(u3, dw_w, dw_b, ln_g, ln_b).reshape(bsz * seq, d)
        return yc, u3[:, seq - hist:, :]

    yp, kp, vp, cp, _ = _layer(x_prompt.reshape(bsz * seq, d), 512, norms, w_bf16, attend_prompt, conv_prompt, BF16,
                               copy_weights=False)

    heads = (N_HEADS, HEAD_DIM)
    return (yp.reshape(bsz, seq, d), ys.reshape(n_s, 1, d),
            kp.reshape(1, bsz, seq, *heads), vp.reshape(1, bsz, seq, *heads), cp[None],
            ks.reshape(1, n_s, 1, *heads), vs.reshape(1, n_s, 1, *heads), cs[None])
```
